```python
import math
import jax, jax.numpy as jnp
from jax import lax
import numpy as np

D_MODEL = 1024
BATCH = 4
SEQ = 8192
DEPTH = 1

D_A = D_MODEL
D_B = D_MODEL
D_MIX = D_A + D_B
N_GROUPS_A = 8
N_HEADS_B = 8
HEAD_DIM_B = D_B // N_HEADS_B
CHUNK = 128
SHORT_CONV = 3
FILTER_EMB = 33
FILTER_HIDDEN = 64
DECAY_TARGET = 1e-2
FAST_DECAY_PCT = 0.3
SLOW_DECAY_PCT = 1.5
D_IN = 4 * D_A + 3 * D_B
EPS = 1e-6

kernel_name = "hyena_gmlp_parallel_hybrid_block"


def rms_norm(x, w):
    xf = x.astype(jnp.float32)
    y = xf * lax.rsqrt(jnp.mean(xf * xf, axis=-1, keepdims=True) + EPS)
    return (y * w.astype(jnp.float32)).astype(x.dtype)


def short_conv_centred(u, w, b):
    up = jnp.pad(u, ((0, 0), (1, 1), (0, 0)))
    return up[:, :-2] * w[0] + up[:, 1:-1] * w[1] + up[:, 2:] * w[2] + b


def hyena_pos_features(L):
    bands = (FILTER_EMB - 1) // 2
    t = jnp.linspace(0.0, 1.0, L, dtype=jnp.float32)[:, None]
    w = 2.0 * math.pi * jnp.arange(L, dtype=jnp.float32)[:, None] / L
    f = jnp.linspace(1e-4, bands - 1, bands, dtype=jnp.float32)[None, :]
    return jnp.concatenate([t, jnp.cos(f * w), -jnp.sin(f * w)], axis=-1), t


def hyena_filter(L, w1, b1, fr1, w2, b2, fr2, w3, b3, fr3, w_o):
    f32 = jnp.float32
    z, t = hyena_pos_features(L)
    h = jnp.sin(fr1.astype(f32) * (z @ w1.astype(f32) + b1.astype(f32)))
    h = jnp.sin(fr2.astype(f32) * (h @ w2.astype(f32) + b2.astype(f32)))
    h = jnp.sin(fr3.astype(f32) * (h @ w3.astype(f32) + b3.astype(f32)))
    h = h @ w_o.astype(f32)
    deltas = jnp.abs(jnp.linspace(math.log(DECAY_TARGET) / SLOW_DECAY_PCT,
                                  math.log(DECAY_TARGET) / FAST_DECAY_PCT,
                                  D_A, dtype=f32))
    decay = jnp.exp(-t * deltas[None, :])
    h_fwd = h[:, :D_A] * decay
    h_bwd = h[:, D_A:] * decay
    l1 = jnp.sum(jnp.abs(h_fwd), axis=0) + jnp.sum(jnp.abs(h_bwd[1:]), axis=0)
    k = jnp.concatenate([h_fwd, jnp.zeros((1, D_A), f32), h_bwd[1:][::-1]], axis=0)
    return k / (l1[None, :] + EPS)


def long_conv_bidir(u, k, skip):
    L = u.shape[1]
    uf = u.astype(jnp.float32)
    U = jnp.fft.rfft(uf, n=2 * L, axis=1)
    K = jnp.fft.rfft(k, axis=0)
    y = jnp.fft.irfft(U * K[None], n=2 * L, axis=1)[:, :L]
    return (y + uf * skip.astype(jnp.float32)).astype(u.dtype)


def setup_inputs(seed: int = 0) -> dict:
    key = jax.random.key(seed)
    ks = jax.random.split(key, 24)
    n = jax.random.normal
    f32 = jnp.float32
    return {
        "x": n(ks[0], (BATCH, SEQ, D_MODEL), f32),
        "pre_norm_w": 1.0 + 0.05 * n(ks[1], (D_MODEL,), f32),
        "w_in": n(ks[2], (D_MODEL, D_IN), f32) * D_MODEL ** -0.5,
        "conv_w": n(ks[3], (SHORT_CONV, 3 * D_A), f32) * SHORT_CONV ** -0.5,
        "conv_b": 0.01 * n(ks[4], (3 * D_A,), f32),
        "filt_w1": n(ks[5], (FILTER_EMB, FILTER_HIDDEN), f32) * FILTER_EMB ** -0.5,
        "filt_b1": 0.1 * n(ks[6], (FILTER_HIDDEN,), f32),
        "filt_freq1": 1.0 + 0.05 * n(ks[7], (FILTER_HIDDEN,), f32),
        "filt_w2": n(ks[8], (FILTER_HIDDEN, FILTER_HIDDEN), f32) * FILTER_HIDDEN ** -0.5,
        "filt_b2": 0.1 * n(ks[9], (FILTER_HIDDEN,), f32),
        "filt_freq2": 1.0 + 0.05 * n(ks[10], (FILTER_HIDDEN,), f32),
        "filt_w3": n(ks[11], (FILTER_HIDDEN, FILTER_HIDDEN), f32) * FILTER_HIDDEN ** -0.5,
        "filt_b3": 0.1 * n(ks[12], (FILTER_HIDDEN,), f32),
        "filt_freq3": 1.0 + 0.05 * n(ks[13], (FILTER_HIDDEN,), f32),
        "filt_w_out": n(ks[14], (FILTER_HIDDEN, 2 * D_A), f32) * FILTER_HIDDEN ** -0.5,
        "hyena_skip": 0.1 * n(ks[15], (D_A,), f32),
        "sgu_norm_w": 1.0 + 0.05 * n(ks[16], (D_B,), f32),
        "sgu_norm_b": 0.01 * n(ks[17], (D_B,), f32),
        "sgu_w": n(ks[18], (N_HEADS_B, CHUNK, CHUNK), f32) * CHUNK ** -0.5,
        "sgu_b": 1.0 + 0.05 * n(ks[19], (N_HEADS_B, CHUNK), f32),
        "w_out": n(ks[20], (D_MIX, D_MODEL), f32) * D_MIX ** -0.5,
        "post_norm_w": 1.0 + 0.05 * n(ks[21], (D_MODEL,), f32),
    }


def reference(x, pre_norm_w, w_in, conv_w, conv_b, filt_w1, filt_b1, filt_freq1,
              filt_w2, filt_b2, filt_freq2, filt_w3, filt_b3, filt_freq3, filt_w_out,
              hyena_skip, sgu_norm_w, sgu_norm_b, sgu_w, sgu_b, w_out, post_norm_w):
    B, L, _ = x.shape
    for _layer in range(DEPTH):
        h = rms_norm(x, pre_norm_w)
        proj = h @ w_in
        hy_in, hy_gate, sg_u, sg_v, sg_gate = jnp.split(
            proj, [3 * D_A, 4 * D_A, 4 * D_A + D_B, 4 * D_A + 2 * D_B], axis=-1)

        hy_in = short_conv_centred(hy_in, conv_w, conv_b)
        x0, x1, v = jnp.split(hy_in, 3, axis=-1)
        k = hyena_filter(L, filt_w1, filt_b1, filt_freq1, filt_w2, filt_b2, filt_freq2,
                         filt_w3, filt_b3, filt_freq3, filt_w_out)
        y_a = x0 * long_conv_bidir(v * x1, k, hyena_skip)
        y_a = y_a * jax.nn.silu(hy_gate)

        vf = sg_v.astype(jnp.float32).reshape(B, L, N_HEADS_B, HEAD_DIM_B)
        mu = jnp.mean(vf, axis=-1, keepdims=True)
        var = jnp.mean(jnp.square(vf - mu), axis=-1, keepdims=True)
        vn = ((vf - mu) * lax.rsqrt(var + EPS)
              * sgu_norm_w.astype(jnp.float32).reshape(N_HEADS_B, HEAD_DIM_B)
              + sgu_norm_b.astype(jnp.float32).reshape(N_HEADS_B, HEAD_DIM_B)).astype(x.dtype)
        vc = vn.reshape(B, L // CHUNK, CHUNK, N_HEADS_B, HEAD_DIM_B)
        mixed = jnp.einsum('hpq,bnqhd->bnphd', sgu_w, vc) \
            + jnp.transpose(sgu_b)[None, None, :, :, None]
        y_b = sg_u * mixed.reshape(B, L, D_B) * jax.nn.silu(sg_gate)

        y = jnp.concatenate([y_a, y_b], axis=-1) @ w_out
        x = x + rms_norm(y, post_norm_w)
    return x
```

```python
import functools
import math

import numpy as np
import jax
import jax.numpy as jnp
from jax import lax
from jax.experimental import pallas as pl
from jax.experimental.pallas import tpu as pltpu

F32 = jnp.float32
BF16 = jnp.bfloat16

EPS = 1e-6
DECAY_TARGET = 1e-2
FAST_DECAY_PCT = 0.3
SLOW_DECAY_PCT = 1.5
FILTER_EMB = 33
EMB_PAD = 64

NB = 128
N_FFT = NB * NB
K1_COUNT = NB // 2 + 1
IM_ROW = 72
STAGE_ROWS = 144
ROW_PITCH = 152
CH_TILE = 256
LANE = 128
TOKEN_TILE = 512
HALO = 8
FILTER_TILE = 1024
VMEM_LIMIT = 60 * 1024 * 1024


@functools.lru_cache(maxsize=None)
def _dft_tables():
    k1 = np.arange(K1_COUNT, dtype=np.int64)
    n1 = np.arange(NB, dtype=np.int64)
    n2 = np.arange(NB, dtype=np.int64)
    phase = (k1[None, :, None] * (NB * n1[None, None, :] + n2[:, None, None])) % N_FFT
    theta = 2.0 * np.pi * phase.astype(np.float64) / N_FFT
    g = np.zeros((NB, STAGE_ROWS, NB), np.float64)
    g[:, :K1_COUNT, :] = np.cos(theta)
    g[:, IM_ROW:IM_ROW + K1_COUNT, :] = -np.sin(theta)
    herm = np.full((K1_COUNT,), 2.0)
    herm[0] = 1.0
    herm[-1] = 1.0
    row_w = np.zeros((STAGE_ROWS,), np.float64)
    row_w[:K1_COUNT] = herm / N_FFT
    row_w[IM_ROW:IM_ROW + K1_COUNT] = herm / N_FFT
    g_inv = np.transpose(g[:, :, :NB // 2] * row_w[None, :, None], (0, 2, 1))
    a = np.arange(NB, dtype=np.int64)
    ang = 2.0 * np.pi * ((a[:, None] * a[None, :]) % NB).astype(np.float64) / NB
    c, s = np.cos(ang), np.sin(ang)
    f_fwd = np.block([[c, s], [-s, c]])
    f_inv = np.block([[c, -s], [s, c]])
    return (g.astype(np.float32), g_inv.astype(np.float32),
            f_fwd.astype(np.float32), f_inv.astype(np.float32))


def _dot(a, b):
    return jnp.dot(a, b, preferred_element_type=F32)


def _dot_split(a, b):
    a_hi = a.astype(BF16)
    b_hi = b.astype(BF16)
    a_lo = (a - a_hi.astype(F32)).astype(BF16)
    b_lo = (b - b_hi.astype(F32)).astype(BF16)
    return _dot(a_hi, b_hi) + _dot(a_lo, b_hi) + _dot(a_hi, b_lo)


def _silu(x):
    return x / (1.0 + jnp.exp(-x))


def _rms_norm(x, w):
    return x * lax.rsqrt(jnp.mean(x * x, axis=-1, keepdims=True) + EPS) * w


def _filter_mlp_kernel(z_ref, t_ref, delta_ref, w1_ref, b1_ref, f1_ref, w2_ref, b2_ref,
                       f2_ref, w3_ref, b3_ref, f3_ref, wo_ref, kc_ref, l1_ref, *, seq_len):
    half = pl.program_id(0)
    tile = pl.program_id(1)
    h = jnp.sin(f1_ref[...] * (_dot_split(z_ref[...], w1_ref[...]) + b1_ref[...]))
    h = jnp.sin(f2_ref[...] * (_dot_split(h, w2_ref[...]) + b2_ref[...]))
    h = jnp.sin(f3_ref[...] * (_dot_split(h, w3_ref[...]) + b3_ref[...]))
    val = _dot_split(h, wo_ref[...]) * jnp.exp(-t_ref[...] * delta_ref[...])
    row = lax.broadcasted_iota(jnp.int32, val.shape, 0) + (half * seq_len + tile * FILTER_TILE)
    val = jnp.where(row == seq_len, 0.0, val)
    for ct in range(kc_ref.shape[0]):
        kc_ref[ct] = val[:, ct * CH_TILE:(ct + 1) * CH_TILE].astype(BF16)

    @pl.when(jnp.logical_and(half == 0, tile == 0))
    def _():
        l1_ref[...] = jnp.zeros_like(l1_ref)

    l1_ref[...] += jnp.sum(jnp.abs(val), axis=0, keepdims=True)


def _filter_mlp(z2, t2, deltas, w1, b1, f1, w2, b2, f2, w3, b3, f3, wo, seq_len, d_a):
    n_ct = d_a // CH_TILE
    tiles = seq_len // FILTER_TILE
    hid = w2.shape[0]
    const = lambda shape: pl.BlockSpec(shape, lambda hf, i: (0,) * len(shape))
    return pl.pallas_call(
        functools.partial(_filter_mlp_kernel, seq_len=seq_len),
        grid=(2, tiles),
        in_specs=[
            pl.BlockSpec((FILTER_TILE, EMB_PAD), lambda hf, i: (hf * tiles + i, 0)),
            pl.BlockSpec((FILTER_TILE, 1), lambda hf, i: (hf * tiles + i, 0)),
            const((1, d_a)),
            const((EMB_PAD, hid)), const((1, hid)), const((1, hid)),
            const((hid, hid)), const((1, hid)), const((1, hid)),
            const((hid, hid)), const((1, hid)), const((1, hid)),
            pl.BlockSpec((hid, d_a), lambda hf, i: (0, hf)),
        ],
        out_specs=[
            pl.BlockSpec((n_ct, FILTER_TILE, CH_TILE), lambda hf, i: (0, hf * tiles + i, 0)),
            pl.BlockSpec((1, d_a), lambda hf, i: (0, 0)),
        ],
        out_shape=[
            jax.ShapeDtypeStruct((n_ct, 2 * seq_len, CH_TILE), BF16),
            jax.ShapeDtypeStruct((1, d_a), F32),
        ],
        compiler_params=pltpu.CompilerParams(
            dimension_semantics=("arbitrary", "arbitrary")),
        name="filter_mlp",
    )(z2, t2, deltas, w1, b1, f1, w2, b2, f2, w3, b3, f3, wo)


def _stage_one(src_ref, g_ref, s_ref):
    def body(n2, carry):
        col = pl.multiple_of(n2 * CH_TILE, CH_TILE)
        row = pl.multiple_of(n2 * ROW_PITCH, 8)
        res = _dot(g_ref[n2], src_ref[:, pl.ds(col, CH_TILE)])
        s_ref[0, pl.ds(row, STAGE_ROWS), :] = res[:, :LANE]
        s_ref[1, pl.ds(row, STAGE_ROWS), :] = res[:, LANE:]
        return carry

    lax.fori_loop(0, NB, body, 0)


def _load_k1(s_ref, k1):
    parts = []
    for off in (0, IM_ROW):
        parts.append(jnp.concatenate(
            [s_ref.at[half][pl.ds(k1 + off, NB, stride=ROW_PITCH), :] for half in range(2)],
            axis=1))
    return jnp.concatenate(parts, axis=0)


def _store_k1(s_ref, k1, val):
    for i, off in enumerate((0, IM_ROW)):
        for half in range(2):
            s_ref.at[half][pl.ds(k1 + off, NB, stride=ROW_PITCH), :] = (
                val[i * NB:(i + 1) * NB, half * LANE:(half + 1) * LANE])


def _filter_fft_kernel(kc_ref, g_ref, f_ref, l1_ref, khat_ref, s_ref):
    _stage_one(kc_ref, g_ref, s_ref)
    scale = 1.0 / (l1_ref[...] + EPS)

    def body(k1, carry):
        x = _dot(f_ref[...], _load_k1(s_ref, k1).astype(BF16))
        khat_ref[k1] = (x * scale).astype(BF16)
        return carry

    lax.fori_loop(0, K1_COUNT, body, 0)


def _filter_fft(kc, g_full, f_fwd, l1):
    n_ct = kc.shape[0]
    kc_v = kc.reshape(n_ct, NB, NB * CH_TILE)
    return pl.pallas_call(
        _filter_fft_kernel,
        grid=(n_ct,),
        in_specs=[
            pl.BlockSpec((None, NB, NB * CH_TILE), lambda c: (c, 0, 0)),
            pl.BlockSpec((NB, STAGE_ROWS, NB), lambda c: (0, 0, 0),
                         pipeline_mode=pl.Buffered(1)),
            pl.BlockSpec((2 * NB, 2 * NB), lambda c: (0, 0)),
            pl.BlockSpec((1, CH_TILE), lambda c: (0, c)),
        ],
        out_specs=pl.BlockSpec((None, K1_COUNT, 2 * NB, CH_TILE), lambda c: (c, 0, 0, 0)),
        out_shape=jax.ShapeDtypeStruct((n_ct, K1_COUNT, 2 * NB, CH_TILE), BF16),
        scratch_shapes=[pltpu.VMEM((2, NB * ROW_PITCH, LANE), F32)],
        compiler_params=pltpu.CompilerParams(
            dimension_semantics=("arbitrary",), vmem_limit_bytes=VMEM_LIMIT),
        name="filter_fft",
    )(kc_v, g_full, f_fwd, l1)


def _long_conv_kernel(u_ref, khat_ref, g_ref, ginv_ref, f_ref, finv_ref, skip_ref,
                      o_ref, s_ref):
    _stage_one(u_ref, g_ref, s_ref)

    def spectrum_body(k1, carry):
        x = _dot(f_ref[...], _load_k1(s_ref, k1).astype(BF16))
        kh = khat_ref[k1].astype(F32)
        xr, xi = x[:NB], x[NB:]
        kr, ki = kh[:NB], kh[NB:]
        y = jnp.concatenate([xr * kr - xi * ki, xr * ki + xi * kr], axis=0)
        _store_k1(s_ref, k1, _dot(finv_ref[...], y.astype(BF16)))
        return carry

    lax.fori_loop(0, K1_COUNT, spectrum_body, 0)

    skip = skip_ref[...]

    def inverse_body(n2, carry):
        col = pl.multiple_of(n2 * CH_TILE, CH_TILE)
        row = pl.multiple_of(n2 * ROW_PITCH, 8)
        b = jnp.concatenate(
            [s_ref[0, pl.ds(row, STAGE_ROWS), :], s_ref[1, pl.ds(row, STAGE_ROWS), :]], axis=1)
        y = _dot(ginv_ref[n2], b.astype(BF16))
        y = y + skip * u_ref[:, pl.ds(col, CH_TILE)].astype(F32)
        o_ref[:, pl.ds(col, CH_TILE)] = y.astype(BF16)
        return carry

    lax.fori_loop(0, NB, inverse_body, 0)


def _long_conv(u, khat, g_data, g_inv, f_fwd, f_inv, skip):
    batch, n_ct, seq_len, _ = u.shape
    rows = seq_len // NB
    u_v = u.reshape(batch, n_ct, rows, NB * CH_TILE)
    const = lambda shape: pl.BlockSpec(shape, lambda c, b: (0,) * len(shape),
                                       pipeline_mode=pl.Buffered(1))
    out = pl.pallas_call(
        _long_conv_kernel,
        grid=(n_ct, batch),
        in_specs=[
            pl.BlockSpec((None, None, rows, NB * CH_TILE), lambda c, b: (b, c, 0, 0)),
            pl.BlockSpec((None, K1_COUNT, 2 * NB, CH_TILE), lambda c, b: (c, 0, 0, 0),
                         pipeline_mode=pl.Buffered(1)),
            const((NB, STAGE_ROWS, rows)),
            const((NB, rows, STAGE_ROWS)),
            const((2 * NB, 2 * NB)),
            const((2 * NB, 2 * NB)),
            pl.BlockSpec((1, CH_TILE), lambda c, b: (0, c)),
        ],
        out_specs=pl.BlockSpec((None, None, rows, NB * CH_TILE), lambda c, b: (b, c, 0, 0)),
        out_shape=jax.ShapeDtypeStruct(u_v.shape, BF16),
        scratch_shapes=[pltpu.VMEM((2, NB * ROW_PITCH, LANE), F32)],
        compiler_params=pltpu.CompilerParams(
            dimension_semantics=("arbitrary", "arbitrary"), vmem_limit_bytes=VMEM_LIMIT),
        name="long_conv",
    )(u_v, khat, g_data, g_inv, f_fwd, f_inv, skip)
    return out.reshape(u.shape)


def _in_proj_kernel(x_ref, xp_ref, xn_ref, pnw_ref, wa_ref, wb_ref, cw_ref, cb_ref,
                    lnw_ref, lnb_ref, sw_ref, sb_ref, u_ref, ga_ref, yb_ref,
                    *, n_ct, n_heads, head_dim, chunk):
    i = pl.program_id(1)
    last = pl.num_programs(1) - 1
    tm = x_ref.shape[0]
    pnw = pnw_ref[...]
    h_main = _rms_norm(x_ref[...], pnw)
    h_prev = jnp.where(i == 0, 0.0, _rms_norm(xp_ref[...], pnw))
    h_next = jnp.where(i == last, 0.0, _rms_norm(xn_ref[...], pnw))
    h_ext = jnp.concatenate([h_prev, h_main, h_next], axis=0).astype(BF16)
    rows = tm + 2 * HALO

    for ct in range(n_ct):
        p = _dot(h_ext, wa_ref[:, ct * 4 * CH_TILE:(ct + 1) * 4 * CH_TILE])
        p_prev = pltpu.roll(p, 1, axis=0)[HALO:HALO + tm]
        p_next = pltpu.roll(p, rows - 1, axis=0)[HALO:HALO + tm]
        p_mid = p[HALO:HALO + tm]
        conv = []
        for which in range(3):
            lo = which * CH_TILE
            cw = cw_ref[which * 3:(which + 1) * 3, ct * CH_TILE:(ct + 1) * CH_TILE]
            cb = cb_ref[which:which + 1, ct * CH_TILE:(ct + 1) * CH_TILE]
            conv.append(p_prev[:, lo:lo + CH_TILE] * cw[0:1]
                        + p_mid[:, lo:lo + CH_TILE] * cw[1:2]
                        + p_next[:, lo:lo + CH_TILE] * cw[2:3] + cb)
        gate = p_mid[:, 3 * CH_TILE:4 * CH_TILE]
        u_ref[ct] = (conv[2] * conv[1]).astype(BF16)
        ga_ref[ct] = (conv[0] * _silu(gate)).astype(BF16)

    h_bf = h_main.astype(BF16)
    pair = 2 * head_dim
    n_chunks = tm // chunk
    for hp in range(n_heads // 2):
        q = _dot(h_bf, wb_ref[:, hp * 3 * pair:(hp + 1) * 3 * pair])
        for sub in range(2):
            hd = hp * 2 + sub
            lo = sub * head_dim
            su = q[:, lo:lo + head_dim]
            sv = q[:, pair + lo:pair + lo + head_dim]
            sg = q[:, 2 * pair + lo:2 * pair + lo + head_dim]
            mu = jnp.mean(sv, axis=-1, keepdims=True)
            d = sv - mu
            var = jnp.mean(d * d, axis=-1, keepdims=True)
            vn = (d * lax.rsqrt(var + EPS) * lnw_ref[:, hd * head_dim:(hd + 1) * head_dim]
                  + lnb_ref[:, hd * head_dim:(hd + 1) * head_dim]).astype(BF16)
            wide = jnp.concatenate(
                [vn[c * chunk:(c + 1) * chunk] for c in range(n_chunks)], axis=1)
            mixed = _dot(sw_ref[hd], wide) + sb_ref[hd]
            mixed = jnp.concatenate(
                [mixed[:, c * head_dim:(c + 1) * head_dim] for c in range(n_chunks)], axis=0)
            yb_ref[:, hd * head_dim:(hd + 1) * head_dim] = (su * mixed * _silu(sg)).astype(BF16)


def _in_proj(x, pre_norm_w, wa, wb, cw, cb, lnw, lnb, sw, sb, n_ct, n_heads, head_dim, chunk):
    batch, seq_len, d_model = x.shape
    tm = TOKEN_TILE
    steps = seq_len // tm
    halo_blocks = seq_len // HALO
    per_tile = tm // HALO
    const = lambda shape: pl.BlockSpec(shape, lambda b, i: (0,) * len(shape))
    const1 = lambda shape: pl.BlockSpec(shape, lambda b, i: (0,) * len(shape),
                                        pipeline_mode=pl.Buffered(1))
    d_b = n_heads * head_dim
    return pl.pallas_call(
        functools.partial(_in_proj_kernel, n_ct=n_ct, n_heads=n_heads, head_dim=head_dim,
                          chunk=chunk),
        grid=(batch, steps),
        in_specs=[
            pl.BlockSpec((None, tm, d_model), lambda b, i: (b, i, 0)),
            pl.BlockSpec((None, HALO, d_model),
                         lambda b, i: (b, jnp.maximum(i * per_tile - 1, 0), 0)),
            pl.BlockSpec((None, HALO, d_model),
                         lambda b, i: (b, jnp.minimum((i + 1) * per_tile, halo_blocks - 1), 0)),
            const((1, d_model)),
            const1(wa.shape), const1(wb.shape),
            const(cw.shape), const(cb.shape), const(lnw.shape), const(lnb.shape),
            const(sw.shape), const(sb.shape),
        ],
        out_specs=[
            pl.BlockSpec((None, n_ct, tm, CH_TILE), lambda b, i: (b, 0, i, 0)),
            pl.BlockSpec((None, n_ct, tm, CH_TILE), lambda b, i: (b, 0, i, 0)),
            pl.BlockSpec((None, tm, d_b), lambda b, i: (b, i, 0)),
        ],
        out_shape=[
            jax.ShapeDtypeStruct((batch, n_ct, seq_len, CH_TILE), BF16),
            jax.ShapeDtypeStruct((batch, n_ct, seq_len, CH_TILE), BF16),
            jax.ShapeDtypeStruct((batch, seq_len, d_b), BF16),
        ],
        compiler_params=pltpu.CompilerParams(
            dimension_semantics=("arbitrary", "arbitrary"), vmem_limit_bytes=VMEM_LIMIT),
        name="in_proj",
    )(x, x, x, pre_norm_w, wa, wb, cw, cb, lnw, lnb, sw, sb)


def _out_proj_kernel(c_ref, ga_ref, yb_ref, x_ref, w_ref, pw_ref, o_ref, *, n_ct):
    ya = [(c_ref[ct].astype(F32) * ga_ref[ct].astype(F32)).astype(BF16) for ct in range(n_ct)]
    yc = jnp.concatenate(ya + [yb_ref[...]], axis=1)
    y = _dot(yc, w_ref[...])
    o_ref[...] = x_ref[...] + _rms_norm(y, pw_ref[...])


def _out_proj(conv, ga, yb, x, w_out, post_norm_w):
    batch, seq_len, d_model = x.shape
    n_ct = conv.shape[1]
    tm = TOKEN_TILE
    return pl.pallas_call(
        functools.partial(_out_proj_kernel, n_ct=n_ct),
        grid=(batch, seq_len // tm),
        in_specs=[
            pl.BlockSpec((None, n_ct, tm, CH_TILE), lambda b, i: (b, 0, i, 0)),
            pl.BlockSpec((None, n_ct, tm, CH_TILE), lambda b, i: (b, 0, i, 0)),
            pl.BlockSpec((None, tm, yb.shape[-1]), lambda b, i: (b, i, 0)),
            pl.BlockSpec((None, tm, d_model), lambda b, i: (b, i, 0)),
            pl.BlockSpec(w_out.shape, lambda b, i: (0, 0)),
            pl.BlockSpec((1, d_model), lambda b, i: (0, 0)),
        ],
        out_specs=pl.BlockSpec((None, tm, d_model), lambda b, i: (b, i, 0)),
        out_shape=jax.ShapeDtypeStruct(x.shape, x.dtype),
        compiler_params=pltpu.CompilerParams(
            dimension_semantics=("arbitrary", "arbitrary"), vmem_limit_bytes=VMEM_LIMIT),
        name="out_proj",
    )(conv, ga, yb, x, w_out, post_norm_w)


def _pos_features(seq_len):
    bands = (FILTER_EMB - 1) // 2
    t = jnp.linspace(0.0, 1.0, seq_len, dtype=F32)[:, None]
    w = 2.0 * math.pi * jnp.arange(seq_len, dtype=F32)[:, None] / seq_len
    f = jnp.linspace(1e-4, bands - 1, bands, dtype=F32)[None, :]
    return jnp.concatenate([t, jnp.cos(f * w), -jnp.sin(f * w)], axis=-1), t


def kernel(x, pre_norm_w, w_in, conv_w, conv_b, filt_w1, filt_b1, filt_freq1, filt_w2, filt_b2, filt_freq2, filt_w3, filt_b3, filt_freq3, filt_w_out, hyena_skip, sgu_norm_w, sgu_norm_b, sgu_w, sgu_b, w_out, post_norm_w):
    batch, seq_len, d_model = x.shape
    d_a = hyena_skip.shape[0]
    n_heads, chunk, _ = sgu_w.shape
    d_b = sgu_norm_w.shape[0]
    head_dim = d_b // n_heads
    n_ct = d_a // CH_TILE
    assert 2 * seq_len == N_FFT and d_a % CH_TILE == 0 and head_dim == LANE
    assert seq_len % TOKEN_TILE == 0 and TOKEN_TILE % chunk == 0 and n_heads % 2 == 0
    assert w_in.shape[1] == 4 * d_a + 3 * d_b

    g_full, g_inv, f_fwd, f_inv = (jnp.asarray(t).astype(BF16) for t in _dft_tables())
    g_data = g_full[:, :, :NB // 2]

    z, t = _pos_features(seq_len)
    pos = jnp.concatenate([jnp.arange(seq_len), (seq_len - jnp.arange(seq_len)) % seq_len])
    z2 = jnp.pad(z, ((0, 0), (0, EMB_PAD - FILTER_EMB)))[pos]
    t2 = t[pos]
    deltas = jnp.abs(jnp.linspace(math.log(DECAY_TARGET) / SLOW_DECAY_PCT,
                                  math.log(DECAY_TARGET) / FAST_DECAY_PCT, d_a, dtype=F32))
    row = lambda v: v.astype(F32).reshape(1, -1)
    w1 = jnp.pad(filt_w1.astype(F32), ((0, EMB_PAD - FILTER_EMB), (0, 0)))
    kc, l1 = _filter_mlp(z2, t2, deltas[None, :], w1, row(filt_b1), row(filt_freq1),
                         filt_w2.astype(F32), row(filt_b2), row(filt_freq2),
                         filt_w3.astype(F32), row(filt_b3), row(filt_freq3),
                         filt_w_out.astype(F32), seq_len, d_a)
    khat = _filter_fft(kc, g_full, f_fwd, l1)

    w_bf = w_in.astype(BF16)
    wa = w_bf[:, :4 * d_a].reshape(d_model, 4, n_ct, CH_TILE)
    wa = jnp.transpose(wa, (0, 2, 1, 3)).reshape(d_model, 4 * d_a)
    pair = 2 * head_dim
    wb = w_bf[:, 4 * d_a:].reshape(d_model, 3, n_heads // 2, pair)
    wb = jnp.transpose(wb, (0, 2, 1, 3)).reshape(d_model, 3 * d_b)
    cw = jnp.transpose(conv_w.astype(F32).reshape(3, 3, d_a), (1, 0, 2)).reshape(9, d_a)
    cb = conv_b.astype(F32).reshape(3, d_a)
    sb = jnp.broadcast_to(sgu_b.astype(F32)[:, :, None], (n_heads, chunk, 1))

    u, ga, yb = _in_proj(x, row(pre_norm_w), wa, wb, cw, cb, row(sgu_norm_w),
                         row(sgu_norm_b), sgu_w.astype(BF16), sb,
                         n_ct, n_heads, head_dim, chunk)
    conv = _long_conv(u, khat, g_data, g_inv, f_fwd, f_inv, row(hyena_skip))
    return _out_proj(conv, ga, yb, x, w_out.astype(BF16), row(post_norm_w))
```

```python
import functools
import math

import numpy as np
import jax
import jax.numpy as jnp
from jax import lax
from jax.experimental import pallas as pl
from jax.experimental.pallas import tpu as pltpu

F32 = jnp.float32
BF16 = jnp.bfloat16

EPS = 1e-6
DECAY_TARGET = 1e-2
FAST_DECAY_PCT = 0.3
SLOW_DECAY_PCT = 1.5
FILTER_EMB = 33
EMB_PAD = 64

NB = 128
N_FFT = NB * NB
K1_COUNT = NB // 2 + 1
IM_ROW = 72
STAGE_ROWS = 144
ROW_PITCH = 152
CH_TILE = 256
LANE = 128
TOKEN_TILE = 512
HALO = 8
FILTER_GROUP = 8
STAGE_UNROLL = 16
SPECTRUM_UNROLL = 8
VMEM_LIMIT = 60 * 1024 * 1024


@functools.lru_cache(maxsize=None)
def _dft_tables():
    k1 = np.arange(K1_COUNT, dtype=np.int64)
    n1 = np.arange(NB, dtype=np.int64)
    n2 = np.arange(NB, dtype=np.int64)
    phase = (k1[None, :, None] * (NB * n1[None, None, :] + n2[:, None, None])) % N_FFT
    theta = 2.0 * np.pi * phase.astype(np.float64) / N_FFT
    g = np.zeros((NB, STAGE_ROWS, NB), np.float64)
    g[:, :K1_COUNT, :] = np.cos(theta)
    g[:, IM_ROW:IM_ROW + K1_COUNT, :] = -np.sin(theta)
    herm = np.full((K1_COUNT,), 2.0)
    herm[0] = 1.0
    herm[-1] = 1.0
    row_w = np.zeros((STAGE_ROWS,), np.float64)
    row_w[:K1_COUNT] = herm / N_FFT
    row_w[IM_ROW:IM_ROW + K1_COUNT] = herm / N_FFT
    g_inv = np.transpose(g[:, :, :NB // 2] * row_w[None, :, None], (0, 2, 1))
    a = np.arange(NB, dtype=np.int64)
    ang = 2.0 * np.pi * ((a[:, None] * a[None, :]) % NB).astype(np.float64) / NB
    c, s = np.cos(ang), np.sin(ang)
    f_fwd = np.block([[c, s], [-s, c]])
    f_inv = np.block([[c, -s], [s, c]])
    return tuple(t.astype(np.float32)
                 for t in (g, np.ascontiguousarray(g[:, :, :NB // 2]), g_inv, f_fwd, f_inv))


def _dot(a, b):
    return jnp.dot(a, b, preferred_element_type=F32)


def _dot_split(a, b):
    a_hi = a.astype(BF16)
    b_hi = b.astype(BF16)
    a_lo = (a - a_hi.astype(F32)).astype(BF16)
    b_lo = (b - b_hi.astype(F32)).astype(BF16)
    return _dot(a_hi, b_hi) + _dot(a_lo, b_hi) + _dot(a_hi, b_lo)


def _silu(x):
    return x / (1.0 + jnp.exp(-x))


def _rms_norm(x, w):
    return x * lax.rsqrt(jnp.mean(x * x, axis=-1, keepdims=True) + EPS) * w


def _filter_mlp_kernel(z_ref, t_ref, delta_ref, w1_ref, b1_ref, f1_ref, w2_ref, b2_ref,
                       f2_ref, w3_ref, b3_ref, f3_ref, wo_ref, kc_ref, l1_ref, *, d_a):
    step = pl.program_id(0)
    half_rows = FILTER_GROUP * (NB // 2)
    h = jnp.sin(f1_ref[...] * (_dot_split(z_ref[...], w1_ref[...]) + b1_ref[...]))
    h = jnp.sin(f2_ref[...] * (_dot_split(h, w2_ref[...]) + b2_ref[...]))
    h = jnp.sin(f3_ref[...] * (_dot_split(h, w3_ref[...]) + b3_ref[...]))
    delta = delta_ref[...]
    vf = (_dot_split(h[:half_rows], wo_ref[:, :d_a])
          * jnp.exp(-t_ref[:half_rows, :] * delta))
    vb = (_dot_split(h[half_rows:], wo_ref[:, d_a:])
          * jnp.exp(-t_ref[half_rows:, :] * delta))
    row = lax.broadcasted_iota(jnp.int32, vb.shape, 0)
    vb = jnp.where(jnp.logical_and(row == 0, step == 0), 0.0, vb)
    hb = NB // 2
    for g in range(FILTER_GROUP):
        for ct in range(kc_ref.shape[0]):
            cols = slice(ct * CH_TILE, (ct + 1) * CH_TILE)
            lanes = slice(g * CH_TILE, (g + 1) * CH_TILE)
            kc_ref[ct, :hb, lanes] = vf[g * hb:(g + 1) * hb, cols].astype(BF16)
            kc_ref[ct, hb:, lanes] = vb[g * hb:(g + 1) * hb, cols].astype(BF16)

    @pl.when(step == 0)
    def _():
        l1_ref[...] = jnp.zeros_like(l1_ref)

    l1_ref[...] += (jnp.sum(jnp.abs(vf), axis=0, keepdims=True)
                    + jnp.sum(jnp.abs(vb), axis=0, keepdims=True))


@functools.lru_cache(maxsize=None)
def _filter_position_tables(seq_len, d_a):
    steps = NB // FILTER_GROUP
    step, half, g, n1h = np.meshgrid(np.arange(steps), np.arange(2), np.arange(FILTER_GROUP),
                                     np.arange(NB // 2), indexing="ij")
    n = NB * (half * (NB // 2) + n1h) + step * FILTER_GROUP + g
    pos = np.where(half == 0, n, (2 * seq_len - n) % seq_len).reshape(-1).astype(np.float64)
    bands = (FILTER_EMB - 1) // 2
    t = pos / (seq_len - 1)
    w = 2.0 * math.pi * pos / seq_len
    f = np.linspace(1e-4, bands - 1, bands)
    z = np.zeros((pos.shape[0], EMB_PAD), np.float64)
    z[:, 0] = t
    z[:, 1:1 + bands] = np.cos(f[None, :] * w[:, None])
    z[:, 1 + bands:1 + 2 * bands] = -np.sin(f[None, :] * w[:, None])
    deltas = np.abs(np.linspace(math.log(DECAY_TARGET) / SLOW_DECAY_PCT,
                                math.log(DECAY_TARGET) / FAST_DECAY_PCT, d_a))
    return (z.astype(np.float32), t[:, None].astype(np.float32),
            deltas[None, :].astype(np.float32))


def _filter_mlp(w1, b1, f1, w2, b2, f2, w3, b3, f3, wo, seq_len, d_a):
    n_ct = d_a // CH_TILE
    steps = NB // FILTER_GROUP
    rows = FILTER_GROUP * NB
    hid = w2.shape[0]
    z2, t2, deltas = _filter_position_tables(seq_len, d_a)
    const = lambda shape: pl.BlockSpec(shape, lambda i: (0,) * len(shape))
    return pl.pallas_call(
        functools.partial(_filter_mlp_kernel, d_a=d_a),
        grid=(steps,),
        in_specs=[
            pl.BlockSpec((rows, EMB_PAD), lambda i: (i, 0)),
            pl.BlockSpec((rows, 1), lambda i: (i, 0)),
            const((1, d_a)),
            const((EMB_PAD, hid)), const((1, hid)), const((1, hid)),
            const((hid, hid)), const((1, hid)), const((1, hid)),
            const((hid, hid)), const((1, hid)), const((1, hid)),
            const((hid, 2 * d_a)),
        ],
        out_specs=[
            pl.BlockSpec((n_ct, NB, FILTER_GROUP * CH_TILE), lambda i: (0, 0, i)),
            pl.BlockSpec((1, d_a), lambda i: (0, 0)),
        ],
        out_shape=[
            jax.ShapeDtypeStruct((n_ct, NB, NB * CH_TILE), BF16),
            jax.ShapeDtypeStruct((1, d_a), F32),
        ],
        compiler_params=pltpu.CompilerParams(dimension_semantics=("arbitrary",)),
        name="filter_mlp",
    )(z2, t2, deltas, w1, b1, f1, w2, b2, f2, w3, b3, f3, wo)


def _stage_one(src_ref, g_ref, s_ref):
    def body(n2, carry):
        col = pl.multiple_of(n2 * CH_TILE, CH_TILE)
        row = pl.multiple_of(n2 * ROW_PITCH, 8)
        res = _dot(g_ref[n2], src_ref[:, pl.ds(col, CH_TILE)])
        s_ref[0, pl.ds(row, STAGE_ROWS), :] = res[:, :LANE]
        s_ref[1, pl.ds(row, STAGE_ROWS), :] = res[:, LANE:]
        return carry

    lax.fori_loop(0, NB, body, 0, unroll=STAGE_UNROLL)


def _load_k1(s_ref, k1):
    parts = []
    for off in (0, IM_ROW):
        parts.append(jnp.concatenate(
            [s_ref.at[half][pl.ds(k1 + off, NB, stride=ROW_PITCH), :] for half in range(2)],
            axis=1))
    return jnp.concatenate(parts, axis=0)


def _store_k1(s_ref, k1, val):
    for i, off in enumerate((0, IM_ROW)):
        for half in range(2):
            s_ref.at[half][pl.ds(k1 + off, NB, stride=ROW_PITCH), :] = (
                val[i * NB:(i + 1) * NB, half * LANE:(half + 1) * LANE])


def _filter_fft_kernel(kc_ref, g_ref, f_ref, l1_ref, khat_ref, s_ref):
    _stage_one(kc_ref, g_ref, s_ref)
    scale = 1.0 / (l1_ref[...] + EPS)

    def body(k1, carry):
        x = _dot(f_ref[...], _load_k1(s_ref, k1).astype(BF16))
        khat_ref[k1] = (x * scale).astype(BF16)
        return carry

    lax.fori_loop(0, K1_COUNT, body, 0, unroll=SPECTRUM_UNROLL)


def _filter_fft(kc, g_full, f_fwd, l1):
    n_ct = kc.shape[0]
    return pl.pallas_call(
        _filter_fft_kernel,
        grid=(n_ct,),
        in_specs=[
            pl.BlockSpec((None, NB, NB * CH_TILE), lambda c: (c, 0, 0)),
            pl.BlockSpec((NB, STAGE_ROWS, NB), lambda c: (0, 0, 0),
                         pipeline_mode=pl.Buffered(1)),
            pl.BlockSpec((2 * NB, 2 * NB), lambda c: (0, 0)),
            pl.BlockSpec((1, CH_TILE), lambda c: (0, c)),
        ],
        out_specs=pl.BlockSpec((None, K1_COUNT, 2 * NB, CH_TILE), lambda c: (c, 0, 0, 0)),
        out_shape=jax.ShapeDtypeStruct((n_ct, K1_COUNT, 2 * NB, CH_TILE), BF16),
        scratch_shapes=[pltpu.VMEM((2, NB * ROW_PITCH, LANE), F32)],
        compiler_params=pltpu.CompilerParams(
            dimension_semantics=("arbitrary",), vmem_limit_bytes=VMEM_LIMIT),
        name="filter_fft",
    )(kc, g_full, f_fwd, l1)


def _long_conv_kernel(u_ref, khat_ref, g_ref, ginv_ref, f_ref, finv_ref, skip_ref,
                      o_ref, s_ref):
    _stage_one(u_ref, g_ref, s_ref)

    def spectrum_body(k1, carry):
        x = _dot(f_ref[...], _load_k1(s_ref, k1).astype(BF16))
        kh = khat_ref[k1].astype(F32)
        xr, xi = x[:NB], x[NB:]
        kr, ki = kh[:NB], kh[NB:]
        y = jnp.concatenate([xr * kr - xi * ki, xr * ki + xi * kr], axis=0)
        _store_k1(s_ref, k1, _dot(finv_ref[...], y.astype(BF16)))
        return carry

    lax.fori_loop(0, K1_COUNT, spectrum_body, 0, unroll=SPECTRUM_UNROLL)

    skip = skip_ref[...]

    def inverse_body(n2, carry):
        col = pl.multiple_of(n2 * CH_TILE, CH_TILE)
        row = pl.multiple_of(n2 * ROW_PITCH, 8)
        b = jnp.concatenate(
            [s_ref[0, pl.ds(row, STAGE_ROWS), :], s_ref[1, pl.ds(row, STAGE_ROWS), :]], axis=1)
        y = _dot(ginv_ref[n2], b.astype(BF16))
        y = y + skip * u_ref[:, pl.ds(col, CH_TILE)].astype(F32)
        o_ref[:, pl.ds(col, CH_TILE)] = y.astype(BF16)
        return carry

    lax.fori_loop(0, NB, inverse_body, 0, unroll=STAGE_UNROLL)


def _long_conv(u, khat, g_data, g_inv, f_fwd, f_inv, skip):
    batch, n_ct, seq_len, _ = u.shape
    rows = seq_len // NB
    u_v = u.reshape(batch, n_ct, rows, NB * CH_TILE)
    const = lambda shape: pl.BlockSpec(shape, lambda c, b: (0,) * len(shape),
                                       pipeline_mode=pl.Buffered(1))
    out = pl.pallas_call(
        _long_conv_kernel,
        grid=(n_ct, batch),
        in_specs=[
            pl.BlockSpec((None, None, rows, NB * CH_TILE), lambda c, b: (b, c, 0, 0)),
            pl.BlockSpec((None, K1_COUNT, 2 * NB, CH_TILE), lambda c, b: (c, 0, 0, 0),
                         pipeline_mode=pl.Buffered(1)),
            const((NB, STAGE_ROWS, rows)),
            const((NB, rows, STAGE_ROWS)),
            const((2 * NB, 2 * NB)),
            const((2 * NB, 2 * NB)),
            pl.BlockSpec((1, CH_TILE), lambda c, b: (0, c)),
        ],
        out_specs=pl.BlockSpec((None, None, rows, NB * CH_TILE), lambda c, b: (b, c, 0, 0)),
        out_shape=jax.ShapeDtypeStruct(u_v.shape, BF16),
        scratch_shapes=[pltpu.VMEM((2, NB * ROW_PITCH, LANE), F32)],
        compiler_params=pltpu.CompilerParams(
            dimension_semantics=("arbitrary", "arbitrary"), vmem_limit_bytes=VMEM_LIMIT),
        name="long_conv",
    )(u_v, khat, g_data, g_inv, f_fwd, f_inv, skip)
    return out.reshape(u.shape)


def _in_proj_kernel(x_ref, xp_ref, xn_ref, pnw_ref, wa_ref, wb_ref, cw_ref, cb_ref,
                    lnw_ref, lnb_ref, sw_ref, sb_ref, u_ref, ga_ref, yb_ref,
                    *, n_ct, n_heads, head_dim, chunk):
    i = pl.program_id(1)
    last = pl.num_programs(1) - 1
    tm = x_ref.shape[0]
    pnw = pnw_ref[...]
    h_main = _rms_norm(x_ref[...], pnw)
    h_prev = jnp.where(i == 0, 0.0, _rms_norm(xp_ref[...], pnw))
    h_next = jnp.where(i == last, 0.0, _rms_norm(xn_ref[...], pnw))
    h_ext = jnp.concatenate([h_prev, h_main, h_next], axis=0).astype(BF16)
    rows = tm + 2 * HALO

    for ct in range(n_ct):
        p = _dot(h_ext, wa_ref[:, ct * 4 * CH_TILE:(ct + 1) * 4 * CH_TILE])
        p_prev = pltpu.roll(p, 1, axis=0)[HALO:HALO + tm]
        p_next = pltpu.roll(p, rows - 1, axis=0)[HALO:HALO + tm]
        p_mid = p[HALO:HALO + tm]
        conv = []
        for which in range(3):
            lo = which * CH_TILE
            cw = cw_ref[which * 3:(which + 1) * 3, ct * CH_TILE:(ct + 1) * CH_TILE]
            cb = cb_ref[which:which + 1, ct * CH_TILE:(ct + 1) * CH_TILE]
            conv.append(p_prev[:, lo:lo + CH_TILE] * cw[0:1]
                        + p_mid[:, lo:lo + CH_TILE] * cw[1:2]
                        + p_next[:, lo:lo + CH_TILE] * cw[2:3] + cb)
        gate = p_mid[:, 3 * CH_TILE:4 * CH_TILE]
        u_ref[ct] = (conv[2] * conv[1]).astype(BF16)
        ga_ref[ct] = (conv[0] * _silu(gate)).astype(BF16)

    h_bf = h_main.astype(BF16)
    pair = 2 * head_dim
    n_chunks = tm // chunk
    for hp in range(n_heads // 2):
        q = _dot(h_bf, wb_ref[:, hp * 3 * pair:(hp + 1) * 3 * pair])
        for sub in range(2):
            hd = hp * 2 + sub
            lo = sub * head_dim
            su = q[:, lo:lo + head_dim]
            sv = q[:, pair + lo:pair + lo + head_dim]
            sg = q[:, 2 * pair + lo:2 * pair + lo + head_dim]
            mu = jnp.mean(sv, axis=-1, keepdims=True)
            d = sv - mu
            var = jnp.mean(d * d, axis=-1, keepdims=True)
            vn = (d * lax.rsqrt(var + EPS) * lnw_ref[:, hd * head_dim:(hd + 1) * head_dim]
                  + lnb_ref[:, hd * head_dim:(hd + 1) * head_dim]).astype(BF16)
            wide = jnp.concatenate(
                [vn[c * chunk:(c + 1) * chunk] for c in range(n_chunks)], axis=1)
            mixed = _dot(sw_ref[hd], wide) + sb_ref[hd]
            mixed = jnp.concatenate(
                [mixed[:, c * head_dim:(c + 1) * head_dim] for c in range(n_chunks)], axis=0)
            yb_ref[:, hd * head_dim:(hd + 1) * head_dim] = (su * mixed * _silu(sg)).astype(BF16)


def _in_proj(x, pre_norm_w, wa, wb, cw, cb, lnw, lnb, sw, sb, n_ct, n_heads, head_dim, chunk):
    batch, seq_len, d_model = x.shape
    tm = TOKEN_TILE
    steps = seq_len // tm
    halo_blocks = seq_len // HALO
    per_tile = tm // HALO
    const = lambda shape: pl.BlockSpec(shape, lambda b, i: (0,) * len(shape))
    const1 = lambda shape: pl.BlockSpec(shape, lambda b, i: (0,) * len(shape),
                                        pipeline_mode=pl.Buffered(1))
    d_b = n_heads * head_dim
    return pl.pallas_call(
        functools.partial(_in_proj_kernel, n_ct=n_ct, n_heads=n_heads, head_dim=head_dim,
                          chunk=chunk),
        grid=(batch, steps),
        in_specs=[
            pl.BlockSpec((None, tm, d_model), lambda b, i: (b, i, 0)),
            pl.BlockSpec((None, HALO, d_model),
                         lambda b, i: (b, jnp.maximum(i * per_tile - 1, 0), 0)),
            pl.BlockSpec((None, HALO, d_model),
                         lambda b, i: (b, jnp.minimum((i + 1) * per_tile, halo_blocks - 1), 0)),
            const((1, d_model)),
            const1(wa.shape), const1(wb.shape),
            const(cw.shape), const(cb.shape), const(lnw.shape), const(lnb.shape),
            const(sw.shape), const(sb.shape),
        ],
        out_specs=[
            pl.BlockSpec((None, n_ct, tm, CH_TILE), lambda b, i: (b, 0, i, 0)),
            pl.BlockSpec((None, n_ct, tm, CH_TILE), lambda b, i: (b, 0, i, 0)),
            pl.BlockSpec((None, tm, d_b), lambda b, i: (b, i, 0)),
        ],
        out_shape=[
            jax.ShapeDtypeStruct((batch, n_ct, seq_len, CH_TILE), BF16),
            jax.ShapeDtypeStruct((batch, n_ct, seq_len, CH_TILE), BF16),
            jax.ShapeDtypeStruct((batch, seq_len, d_b), BF16),
        ],
        compiler_params=pltpu.CompilerParams(
            dimension_semantics=("arbitrary", "arbitrary"), vmem_limit_bytes=VMEM_LIMIT),
        name="in_proj",
    )(x, x, x, pre_norm_w, wa, wb, cw, cb, lnw, lnb, sw, sb)


def _out_proj_kernel(c_ref, ga_ref, yb_ref, x_ref, w_ref, pw_ref, o_ref, *, n_ct):
    ya = [(c_ref[ct].astype(F32) * ga_ref[ct].astype(F32)).astype(BF16) for ct in range(n_ct)]
    yc = jnp.concatenate(ya + [yb_ref[...]], axis=1)
    y = _dot(yc, w_ref[...])
    o_ref[...] = x_ref[...] + _rms_norm(y, pw_ref[...])


def _out_proj(conv, ga, yb, x, w_out, post_norm_w):
    batch, seq_len, d_model = x.shape
    n_ct = conv.shape[1]
    tm = TOKEN_TILE
    return pl.pallas_call(
        functools.partial(_out_proj_kernel, n_ct=n_ct),
        grid=(batch, seq_len // tm),
        in_specs=[
            pl.BlockSpec((None, n_ct, tm, CH_TILE), lambda b, i: (b, 0, i, 0)),
            pl.BlockSpec((None, n_ct, tm, CH_TILE), lambda b, i: (b, 0, i, 0)),
            pl.BlockSpec((None, tm, yb.shape[-1]), lambda b, i: (b, i, 0)),
            pl.BlockSpec((None, tm, d_model), lambda b, i: (b, i, 0)),
            pl.BlockSpec(w_out.shape, lambda b, i: (0, 0)),
            pl.BlockSpec((1, d_model), lambda b, i: (0, 0)),
        ],
        out_specs=pl.BlockSpec((None, tm, d_model), lambda b, i: (b, i, 0)),
        out_shape=jax.ShapeDtypeStruct(x.shape, x.dtype),
        compiler_params=pltpu.CompilerParams(
            dimension_semantics=("arbitrary", "arbitrary"), vmem_limit_bytes=VMEM_LIMIT),
        name="out_proj",
    )(conv, ga, yb, x, w_out, post_norm_w)


def kernel(x, pre_norm_w, w_in, conv_w, conv_b, filt_w1, filt_b1, filt_freq1, filt_w2, filt_b2, filt_freq2, filt_w3, filt_b3, filt_freq3, filt_w_out, hyena_skip, sgu_norm_w, sgu_norm_b, sgu_w, sgu_b, w_out, post_norm_w):
    batch, seq_len, d_model = x.shape
    d_a = hyena_skip.shape[0]
    n_heads, chunk, _ = sgu_w.shape
    d_b = sgu_norm_w.shape[0]
    head_dim = d_b // n_heads
    n_ct = d_a // CH_TILE
    assert 2 * seq_len == N_FFT and d_a % CH_TILE == 0 and head_dim == LANE
    assert seq_len % TOKEN_TILE == 0 and TOKEN_TILE % chunk == 0 and n_heads % 2 == 0
    assert w_in.shape[1] == 4 * d_a + 3 * d_b

    g_full, g_data, g_inv, f_fwd, f_inv = (jnp.asarray(t).astype(BF16) for t in _dft_tables())

    row = lambda v: v.astype(F32).reshape(1, -1)
    w1 = jnp.pad(filt_w1.astype(F32), ((0, EMB_PAD - FILTER_EMB), (0, 0)))
    kc, l1 = _filter_mlp(w1, row(filt_b1), row(filt_freq1),
                         filt_w2.astype(F32), row(filt_b2), row(filt_freq2),
                         filt_w3.astype(F32), row(filt_b3), row(filt_freq3),
                         filt_w_out.astype(F32), seq_len, d_a)
    khat = _filter_fft(kc, g_full, f_fwd, l1)

    w_bf = w_in.astype(BF16)
    wa = w_bf[:, :4 * d_a].reshape(d_model, 4, n_ct, CH_TILE)
    wa = jnp.transpose(wa, (0, 2, 1, 3)).reshape(d_model, 4 * d_a)
    pair = 2 * head_dim
    wb = w_bf[:, 4 * d_a:].reshape(d_model, 3, n_heads // 2, pair)
    wb = jnp.transpose(wb, (0, 2, 1, 3)).reshape(d_model, 3 * d_b)
    cw = jnp.transpose(conv_w.astype(F32).reshape(3, 3, d_a), (1, 0, 2)).reshape(9, d_a)
    cb = conv_b.astype(F32).reshape(3, d_a)
    sb = jnp.broadcast_to(sgu_b.astype(F32)[:, :, None], (n_heads, chunk, 1))

    u, ga, yb = _in_proj(x, row(pre_norm_w), wa, wb, cw, cb, row(sgu_norm_w),
                         row(sgu_norm_b), sgu_w.astype(BF16), sb,
                         n_ct, n_heads, head_dim, chunk)
    conv = _long_conv(u, khat, g_data, g_inv, f_fwd, f_inv, row(hyena_skip))
    return _out_proj(conv, ga, yb, x, w_out.astype(BF16), row(post_norm_w))
```

```python
import functools
import math

import numpy as np
import jax
import jax.numpy as jnp
from jax import lax
from jax.experimental import pallas as pl
from jax.experimental.pallas import tpu as pltpu

F32 = jnp.float32
BF16 = jnp.bfloat16

EPS = 1e-6
DECAY_TARGET = 1e-2
FAST_DECAY_PCT = 0.3
SLOW_DECAY_PCT = 1.5
FILTER_EMB = 33
EMB_PAD = 64

NB = 128
N_FFT = NB * NB
K1_COUNT = NB // 2 + 1
IM_ROW = 72
STAGE_ROWS = 144
ROW_PITCH = 152
CH_TILE = 256
LANE = 128
TOKEN_TILE = 512
HALO = 8
FILTER_GROUP = 8
STAGE_UNROLL = 16
SPECTRUM_UNROLL = 8
VMEM_LIMIT = 60 * 1024 * 1024


@functools.lru_cache(maxsize=None)
def _dft_tables():
    k1 = np.arange(K1_COUNT, dtype=np.int64)
    n1 = np.arange(NB, dtype=np.int64)
    n2 = np.arange(NB, dtype=np.int64)
    phase = (k1[None, :, None] * (NB * n1[None, None, :] + n2[:, None, None])) % N_FFT
    theta = 2.0 * np.pi * phase.astype(np.float64) / N_FFT
    g = np.zeros((NB, STAGE_ROWS, NB), np.float64)
    g[:, :K1_COUNT, :] = np.cos(theta)
    g[:, IM_ROW:IM_ROW + K1_COUNT, :] = -np.sin(theta)
    herm = np.full((K1_COUNT,), 2.0)
    herm[0] = 1.0
    herm[-1] = 1.0
    row_w = np.zeros((STAGE_ROWS,), np.float64)
    row_w[:K1_COUNT] = herm / N_FFT
    row_w[IM_ROW:IM_ROW + K1_COUNT] = herm / N_FFT
    g_inv = np.transpose(g[:, :, :NB // 2] * row_w[None, :, None], (0, 2, 1))
    a = np.arange(NB, dtype=np.int64)
    ang = 2.0 * np.pi * ((a[:, None] * a[None, :]) % NB).astype(np.float64) / NB
    c, s = np.cos(ang), np.sin(ang)
    f_fwd = np.block([[c, s], [-s, c]])
    f_inv = np.block([[c, -s], [s, c]])
    half = NB // 2
    g_pair = np.zeros((half, 2, STAGE_ROWS, half, 2), np.float64)
    g_inv_pair = np.zeros((half, half, 2, 2, STAGE_ROWS), np.float64)
    for p in range(2):
        g_pair[:, p, :, :, p] = g[p::2, :, :half]
        g_inv_pair[:, :, p, p, :] = g_inv[p::2]
    g_pair = g_pair.reshape(half, 2 * STAGE_ROWS, NB)
    g_inv_pair = g_inv_pair.reshape(half, NB, 2 * STAGE_ROWS)
    return tuple(t.astype(np.float32) for t in (g, g_pair, g_inv_pair, f_fwd, f_inv))


def _dot(a, b):
    return jnp.dot(a, b, preferred_element_type=F32)


def _dot_split(a, b):
    a_hi = a.astype(BF16)
    b_hi = b.astype(BF16)
    a_lo = (a - a_hi.astype(F32)).astype(BF16)
    b_lo = (b - b_hi.astype(F32)).astype(BF16)
    return _dot(a_hi, b_hi) + _dot(a_lo, b_hi) + _dot(a_hi, b_lo)


def _silu(x):
    return x / (1.0 + jnp.exp(-x))


def _rms_norm(x, w):
    return x * lax.rsqrt(jnp.mean(x * x, axis=-1, keepdims=True) + EPS) * w


def _filter_mlp_kernel(z_ref, t_ref, delta_ref, w1_ref, b1_ref, f1_ref, w2_ref, b2_ref,
                       f2_ref, w3_ref, b3_ref, f3_ref, wo_ref, kc_ref, l1_ref, *, d_a):
    step = pl.program_id(0)
    half_rows = FILTER_GROUP * (NB // 2)
    h = jnp.sin(f1_ref[...] * (_dot_split(z_ref[...], w1_ref[...]) + b1_ref[...]))
    h = jnp.sin(f2_ref[...] * (_dot_split(h, w2_ref[...]) + b2_ref[...]))
    h = jnp.sin(f3_ref[...] * (_dot_split(h, w3_ref[...]) + b3_ref[...]))
    delta = delta_ref[...]
    vf = (_dot_split(h[:half_rows], wo_ref[:, :d_a])
          * jnp.exp(-t_ref[:half_rows, :] * delta))
    vb = (_dot_split(h[half_rows:], wo_ref[:, d_a:])
          * jnp.exp(-t_ref[half_rows:, :] * delta))
    row = lax.broadcasted_iota(jnp.int32, vb.shape, 0)
    vb = jnp.where(jnp.logical_and(row == 0, step == 0), 0.0, vb)
    hb = NB // 2
    for g in range(FILTER_GROUP):
        for ct in range(kc_ref.shape[0]):
            cols = slice(ct * CH_TILE, (ct + 1) * CH_TILE)
            lanes = slice(g * CH_TILE, (g + 1) * CH_TILE)
            kc_ref[ct, :hb, lanes] = vf[g * hb:(g + 1) * hb, cols].astype(BF16)
            kc_ref[ct, hb:, lanes] = vb[g * hb:(g + 1) * hb, cols].astype(BF16)

    @pl.when(step == 0)
    def _():
        l1_ref[...] = jnp.zeros_like(l1_ref)

    l1_ref[...] += (jnp.sum(jnp.abs(vf), axis=0, keepdims=True)
                    + jnp.sum(jnp.abs(vb), axis=0, keepdims=True))


@functools.lru_cache(maxsize=None)
def _filter_position_tables(seq_len, d_a):
    steps = NB // FILTER_GROUP
    step, half, g, n1h = np.meshgrid(np.arange(steps), np.arange(2), np.arange(FILTER_GROUP),
                                     np.arange(NB // 2), indexing="ij")
    n = NB * (half * (NB // 2) + n1h) + step * FILTER_GROUP + g
    pos = np.where(half == 0, n, (2 * seq_len - n) % seq_len).reshape(-1).astype(np.float64)
    bands = (FILTER_EMB - 1) // 2
    t = pos / (seq_len - 1)
    w = 2.0 * math.pi * pos / seq_len
    f = np.linspace(1e-4, bands - 1, bands)
    z = np.zeros((pos.shape[0], EMB_PAD), np.float64)
    z[:, 0] = t
    z[:, 1:1 + bands] = np.cos(f[None, :] * w[:, None])
    z[:, 1 + bands:1 + 2 * bands] = -np.sin(f[None, :] * w[:, None])
    deltas = np.abs(np.linspace(math.log(DECAY_TARGET) / SLOW_DECAY_PCT,
                                math.log(DECAY_TARGET) / FAST_DECAY_PCT, d_a))
    return (z.astype(np.float32), t[:, None].astype(np.float32),
            deltas[None, :].astype(np.float32))


def _filter_mlp(w1, b1, f1, w2, b2, f2, w3, b3, f3, wo, seq_len, d_a):
    n_ct = d_a // CH_TILE
    steps = NB // FILTER_GROUP
    rows = FILTER_GROUP * NB
    hid = w2.shape[0]
    z2, t2, deltas = _filter_position_tables(seq_len, d_a)
    const = lambda shape: pl.BlockSpec(shape, lambda i: (0,) * len(shape))
    return pl.pallas_call(
        functools.partial(_filter_mlp_kernel, d_a=d_a),
        grid=(steps,),
        in_specs=[
            pl.BlockSpec((rows, EMB_PAD), lambda i: (i, 0)),
            pl.BlockSpec((rows, 1), lambda i: (i, 0)),
            const((1, d_a)),
            const((EMB_PAD, hid)), const((1, hid)), const((1, hid)),
            const((hid, hid)), const((1, hid)), const((1, hid)),
            const((hid, hid)), const((1, hid)), const((1, hid)),
            const((hid, 2 * d_a)),
        ],
        out_specs=[
            pl.BlockSpec((n_ct, NB, FILTER_GROUP * CH_TILE), lambda i: (0, 0, i)),
            pl.BlockSpec((1, d_a), lambda i: (0, 0)),
        ],
        out_shape=[
            jax.ShapeDtypeStruct((n_ct, NB, NB * CH_TILE), BF16),
            jax.ShapeDtypeStruct((1, d_a), F32),
        ],
        compiler_params=pltpu.CompilerParams(dimension_semantics=("arbitrary",)),
        name="filter_mlp",
    )(z2, t2, deltas, w1, b1, f1, w2, b2, f2, w3, b3, f3, wo)


def _store_slab(s_ref, n2, res):
    row = pl.multiple_of(n2 * ROW_PITCH, 8)
    s_ref[0, pl.ds(row, STAGE_ROWS), :] = res[:, :LANE]
    s_ref[1, pl.ds(row, STAGE_ROWS), :] = res[:, LANE:]


def _load_slab(s_ref, n2):
    row = pl.multiple_of(n2 * ROW_PITCH, 8)
    return jnp.concatenate(
        [s_ref[0, pl.ds(row, STAGE_ROWS), :], s_ref[1, pl.ds(row, STAGE_ROWS), :]], axis=1)


def _stage_one(src_ref, g_ref, s_ref):
    def body(n2, carry):
        col = pl.multiple_of(n2 * CH_TILE, CH_TILE)
        _store_slab(s_ref, n2, _dot(g_ref[n2], src_ref[:, pl.ds(col, CH_TILE)]))
        return carry

    lax.fori_loop(0, NB, body, 0, unroll=STAGE_UNROLL)


def _pack_rows(x):
    return pltpu.bitcast(x.astype(BF16), jnp.uint32)


def _stage_one_packed(u_ref, g_ref, s_ref):
    def body(m, carry):
        w = jnp.concatenate(
            [u_ref.at[half][pl.ds(m, NB // 2, stride=NB // 2), :] for half in range(2)], axis=1)
        res = _dot(g_ref[m], pltpu.bitcast(w, BF16))
        _store_slab(s_ref, 2 * m, res[:STAGE_ROWS])
        _store_slab(s_ref, 2 * m + 1, res[STAGE_ROWS:])
        return carry

    lax.fori_loop(0, NB // 2, body, 0, unroll=STAGE_UNROLL // 2)


def _load_k1(s_ref, k1):
    parts = []
    for off in (0, IM_ROW):
        parts.append(jnp.concatenate(
            [s_ref.at[half][pl.ds(k1 + off, NB, stride=ROW_PITCH), :] for half in range(2)],
            axis=1))
    return jnp.concatenate(parts, axis=0)


def _store_k1(s_ref, k1, val):
    for i, off in enumerate((0, IM_ROW)):
        for half in range(2):
            s_ref.at[half][pl.ds(k1 + off, NB, stride=ROW_PITCH), :] = (
                val[i * NB:(i + 1) * NB, half * LANE:(half + 1) * LANE])


def _filter_fft_kernel(kc_ref, g_ref, f_ref, l1_ref, skip_ref, khat_ref, s_ref):
    _stage_one(kc_ref, g_ref, s_ref)
    scale = 1.0 / (l1_ref[...] + EPS)
    is_real = lax.broadcasted_iota(jnp.int32, (2 * NB, CH_TILE), 0) < NB
    shift = jnp.where(is_real, skip_ref[...], 0.0)

    def body(k1, carry):
        x = _dot(f_ref[...], _load_k1(s_ref, k1).astype(BF16))
        khat_ref[k1] = (x * scale + shift).astype(BF16)
        return carry

    lax.fori_loop(0, K1_COUNT, body, 0, unroll=SPECTRUM_UNROLL)


def _filter_fft(kc, g_full, f_fwd, l1, skip):
    n_ct = kc.shape[0]
    return pl.pallas_call(
        _filter_fft_kernel,
        grid=(n_ct,),
        in_specs=[
            pl.BlockSpec((None, NB, NB * CH_TILE), lambda c: (c, 0, 0)),
            pl.BlockSpec((NB, STAGE_ROWS, NB), lambda c: (0, 0, 0),
                         pipeline_mode=pl.Buffered(1)),
            pl.BlockSpec((2 * NB, 2 * NB), lambda c: (0, 0)),
            pl.BlockSpec((1, CH_TILE), lambda c: (0, c)),
            pl.BlockSpec((1, CH_TILE), lambda c: (0, c)),
        ],
        out_specs=pl.BlockSpec((None, K1_COUNT, 2 * NB, CH_TILE), lambda c: (c, 0, 0, 0)),
        out_shape=jax.ShapeDtypeStruct((n_ct, K1_COUNT, 2 * NB, CH_TILE), BF16),
        scratch_shapes=[pltpu.VMEM((2, NB * ROW_PITCH, LANE), F32)],
        compiler_params=pltpu.CompilerParams(
            dimension_semantics=("arbitrary",), vmem_limit_bytes=VMEM_LIMIT),
        name="filter_fft",
    )(kc, g_full, f_fwd, l1, skip)


def _long_conv_kernel(u_ref, khat_ref, g_ref, ginv_ref, f_ref, finv_ref, o_ref, s_ref):
    _stage_one_packed(u_ref, g_ref, s_ref)

    def spectrum_body(k1, carry):
        x = _dot(f_ref[...], _load_k1(s_ref, k1).astype(BF16))
        kh = khat_ref[k1].astype(F32)
        xr, xi = x[:NB], x[NB:]
        kr, ki = kh[:NB], kh[NB:]
        y = jnp.concatenate([xr * kr - xi * ki, xr * ki + xi * kr], axis=0)
        _store_k1(s_ref, k1, _dot(finv_ref[...], y.astype(BF16)))
        return carry

    lax.fori_loop(0, K1_COUNT, spectrum_body, 0, unroll=SPECTRUM_UNROLL)

    def inverse_body(m, carry):
        b = jnp.concatenate([_load_slab(s_ref, 2 * m), _load_slab(s_ref, 2 * m + 1)], axis=0)
        w = _pack_rows(_dot(ginv_ref[m], b.astype(BF16)))
        for half in range(2):
            o_ref.at[half][pl.ds(m, NB // 2, stride=NB // 2), :] = (
                w[:, half * LANE:(half + 1) * LANE])
        return carry

    lax.fori_loop(0, NB // 2, inverse_body, 0, unroll=STAGE_UNROLL // 2)


def _long_conv(u, khat, g_pair, g_inv_pair, f_fwd, f_inv):
    batch, n_slab, half_len, _ = u.shape
    slabs = CH_TILE // LANE
    n_ct = n_slab // slabs
    const = lambda shape: pl.BlockSpec(shape, lambda c, b: (0,) * len(shape),
                                       pipeline_mode=pl.Buffered(1))
    return pl.pallas_call(
        _long_conv_kernel,
        grid=(n_ct, batch),
        in_specs=[
            pl.BlockSpec((None, slabs, half_len, LANE), lambda c, b: (b, c, 0, 0)),
            pl.BlockSpec((None, K1_COUNT, 2 * NB, CH_TILE), lambda c, b: (c, 0, 0, 0),
                         pipeline_mode=pl.Buffered(1)),
            const(g_pair.shape),
            const(g_inv_pair.shape),
            const((2 * NB, 2 * NB)),
            const((2 * NB, 2 * NB)),
        ],
        out_specs=pl.BlockSpec((None, slabs, half_len, LANE), lambda c, b: (b, c, 0, 0)),
        out_shape=jax.ShapeDtypeStruct(u.shape, jnp.uint32),
        scratch_shapes=[pltpu.VMEM((2, NB * ROW_PITCH, LANE), F32)],
        compiler_params=pltpu.CompilerParams(
            dimension_semantics=("arbitrary", "arbitrary"), vmem_limit_bytes=VMEM_LIMIT),
        name="long_conv",
    )(u, khat, g_pair, g_inv_pair, f_fwd, f_inv)


def _in_proj_kernel(x_ref, xp_ref, xn_ref, pnw_ref, w_ref, cw_ref, cb_ref,
                    lnw_ref, lnb_ref, sw_ref, sb_ref, u_ref, ga_ref, yb_ref,
                    *, n_ct, n_heads, head_dim, chunk):
    i = pl.program_id(1)
    last = pl.num_programs(1) - 1
    tm = x_ref.shape[0]
    pnw = pnw_ref[...]
    h_main = _rms_norm(x_ref[...], pnw)
    h_prev = jnp.where(i == 0, 0.0, _rms_norm(xp_ref[...], pnw))
    h_next = jnp.where(i == last, 0.0, _rms_norm(xn_ref[...], pnw))
    h_ext = jnp.concatenate([h_prev, h_main, h_next], axis=0).astype(BF16)
    h_bf = h_main.astype(BF16)
    rows = tm + 2 * HALO

    d_a = n_ct * CH_TILE
    for ct in range(n_ct):
        cols = slice(ct * CH_TILE, (ct + 1) * CH_TILE)
        conv = []
        for which in range(3):
            p = _dot(h_ext, w_ref[:, which * d_a + ct * CH_TILE:which * d_a + (ct + 1) * CH_TILE])
            cw = cw_ref[which * 3:(which + 1) * 3, cols]
            conv.append(pltpu.roll(p, 1, axis=0)[HALO:HALO + tm] * cw[0:1]
                        + p[HALO:HALO + tm] * cw[1:2]
                        + pltpu.roll(p, rows - 1, axis=0)[HALO:HALO + tm] * cw[2:3]
                        + cb_ref[which:which + 1, cols])
        gate = _dot(h_bf, w_ref[:, 3 * d_a + ct * CH_TILE:3 * d_a + (ct + 1) * CH_TILE])
        u = conv[2] * conv[1]
        slabs = CH_TILE // LANE
        for half in range(slabs):
            u_ref[ct * slabs + half] = _pack_rows(u[:, half * LANE:(half + 1) * LANE])
        ga_ref[:, ct * CH_TILE:(ct + 1) * CH_TILE] = (conv[0] * _silu(gate)).astype(BF16)

    d_b = n_heads * head_dim
    pair = 2 * head_dim
    n_chunks = tm // chunk
    for hp in range(n_heads // 2):
        q = [_dot(h_bf, w_ref[:, 4 * d_a + which * d_b + hp * pair:
                              4 * d_a + which * d_b + (hp + 1) * pair]) for which in range(3)]
        for sub in range(2):
            hd = hp * 2 + sub
            lo = sub * head_dim
            su = q[0][:, lo:lo + head_dim]
            sv = q[1][:, lo:lo + head_dim]
            sg = q[2][:, lo:lo + head_dim]
            mu = jnp.mean(sv, axis=-1, keepdims=True)
            d = sv - mu
            var = jnp.mean(d * d, axis=-1, keepdims=True)
            vn = (d * lax.rsqrt(var + EPS) * lnw_ref[:, hd * head_dim:(hd + 1) * head_dim]
                  + lnb_ref[:, hd * head_dim:(hd + 1) * head_dim]).astype(BF16)
            wide = jnp.concatenate(
                [vn[c * chunk:(c + 1) * chunk] for c in range(n_chunks)], axis=1)
            mixed = _dot(sw_ref[hd], wide) + sb_ref[hd]
            mixed = jnp.concatenate(
                [mixed[:, c * head_dim:(c + 1) * head_dim] for c in range(n_chunks)], axis=0)
            yb_ref[:, hd * head_dim:(hd + 1) * head_dim] = (su * mixed * _silu(sg)).astype(BF16)


def _in_proj(x, pre_norm_w, w, cw, cb, lnw, lnb, sw, sb, n_ct, n_heads, head_dim, chunk):
    batch, seq_len, d_model = x.shape
    tm = TOKEN_TILE
    steps = seq_len // tm
    halo_blocks = seq_len // HALO
    per_tile = tm // HALO
    const = lambda shape: pl.BlockSpec(shape, lambda b, i: (0,) * len(shape))
    const1 = lambda shape: pl.BlockSpec(shape, lambda b, i: (0,) * len(shape),
                                        pipeline_mode=pl.Buffered(1))
    d_b = n_heads * head_dim
    d_a = n_ct * CH_TILE
    return pl.pallas_call(
        functools.partial(_in_proj_kernel, n_ct=n_ct, n_heads=n_heads, head_dim=head_dim,
                          chunk=chunk),
        grid=(batch, steps),
        in_specs=[
            pl.BlockSpec((None, tm, d_model), lambda b, i: (b, i, 0)),
            pl.BlockSpec((None, HALO, d_model),
                         lambda b, i: (b, jnp.maximum(i * per_tile - 1, 0), 0)),
            pl.BlockSpec((None, HALO, d_model),
                         lambda b, i: (b, jnp.minimum((i + 1) * per_tile, halo_blocks - 1), 0)),
            const((1, d_model)),
            const1(w.shape),
            const(cw.shape), const(cb.shape), const(lnw.shape), const(lnb.shape),
            const(sw.shape), const(sb.shape),
        ],
        out_specs=[
            pl.BlockSpec((None, d_a // LANE, tm // 2, LANE), lambda b, i: (b, 0, i, 0)),
            pl.BlockSpec((None, tm, d_a), lambda b, i: (b, i, 0)),
            pl.BlockSpec((None, tm, d_b), lambda b, i: (b, i, 0)),
        ],
        out_shape=[
            jax.ShapeDtypeStruct((batch, d_a // LANE, seq_len // 2, LANE), jnp.uint32),
            jax.ShapeDtypeStruct((batch, seq_len, d_a), BF16),
            jax.ShapeDtypeStruct((batch, seq_len, d_b), BF16),
        ],
        compiler_params=pltpu.CompilerParams(
            dimension_semantics=("arbitrary", "arbitrary"), vmem_limit_bytes=VMEM_LIMIT),
        name="in_proj",
    )(x, x, x, pre_norm_w, w, cw, cb, lnw, lnb, sw, sb)


def _out_proj_kernel(c_ref, ga_ref, yb_ref, x_ref, w_ref, pw_ref, o_ref):
    conv = jnp.concatenate(
        [pltpu.bitcast(c_ref[s], BF16) for s in range(c_ref.shape[0])], axis=1)
    ya = (conv.astype(F32) * ga_ref[...].astype(F32)).astype(BF16)
    yc = jnp.concatenate([ya, yb_ref[...]], axis=1)
    y = _dot(yc, w_ref[...])
    o_ref[...] = x_ref[...] + _rms_norm(y, pw_ref[...])


def _out_proj(conv, ga, yb, x, w_out, post_norm_w):
    batch, seq_len, d_model = x.shape
    n_slab = conv.shape[1]
    tm = TOKEN_TILE
    return pl.pallas_call(
        _out_proj_kernel,
        grid=(batch, seq_len // tm),
        in_specs=[
            pl.BlockSpec((None, n_slab, tm // 2, LANE), lambda b, i: (b, 0, i, 0)),
            pl.BlockSpec((None, tm, ga.shape[-1]), lambda b, i: (b, i, 0)),
            pl.BlockSpec((None, tm, yb.shape[-1]), lambda b, i: (b, i, 0)),
            pl.BlockSpec((None, tm, d_model), lambda b, i: (b, i, 0)),
            pl.BlockSpec(w_out.shape, lambda b, i: (0, 0)),
            pl.BlockSpec((1, d_model), lambda b, i: (0, 0)),
        ],
        out_specs=pl.BlockSpec((None, tm, d_model), lambda b, i: (b, i, 0)),
        out_shape=jax.ShapeDtypeStruct(x.shape, x.dtype),
        compiler_params=pltpu.CompilerParams(
            dimension_semantics=("arbitrary", "arbitrary"), vmem_limit_bytes=VMEM_LIMIT),
        name="out_proj",
    )(conv, ga, yb, x, w_out, post_norm_w)


def kernel(x, pre_norm_w, w_in, conv_w, conv_b, filt_w1, filt_b1, filt_freq1, filt_w2, filt_b2, filt_freq2, filt_w3, filt_b3, filt_freq3, filt_w_out, hyena_skip, sgu_norm_w, sgu_norm_b, sgu_w, sgu_b, w_out, post_norm_w):
    batch, seq_len, d_model = x.shape
    d_a = hyena_skip.shape[0]
    n_heads, chunk, _ = sgu_w.shape
    d_b = sgu_norm_w.shape[0]
    head_dim = d_b // n_heads
    n_ct = d_a // CH_TILE
    assert 2 * seq_len == N_FFT and d_a % CH_TILE == 0 and head_dim == LANE
    assert seq_len % TOKEN_TILE == 0 and TOKEN_TILE % chunk == 0 and n_heads % 2 == 0
    assert w_in.shape[1] == 4 * d_a + 3 * d_b

    g_full, g_pair, g_inv_pair, f_fwd, f_inv = (
        jnp.asarray(t).astype(BF16) for t in _dft_tables())

    row = lambda v: v.astype(F32).reshape(1, -1)
    w1 = jnp.pad(filt_w1.astype(F32), ((0, EMB_PAD - FILTER_EMB), (0, 0)))
    kc, l1 = _filter_mlp(w1, row(filt_b1), row(filt_freq1),
                         filt_w2.astype(F32), row(filt_b2), row(filt_freq2),
                         filt_w3.astype(F32), row(filt_b3), row(filt_freq3),
                         filt_w_out.astype(F32), seq_len, d_a)
    khat = _filter_fft(kc, g_full, f_fwd, l1, row(hyena_skip))

    cw = jnp.transpose(conv_w.astype(F32).reshape(3, 3, d_a), (1, 0, 2)).reshape(9, d_a)
    cb = conv_b.astype(F32).reshape(3, d_a)
    sb = jnp.broadcast_to(sgu_b.astype(F32)[:, :, None], (n_heads, chunk, 1))

    u, ga, yb = _in_proj(x, row(pre_norm_w), w_in.astype(BF16), cw, cb, row(sgu_norm_w),
                         row(sgu_norm_b), sgu_w.astype(BF16), sb,
                         n_ct, n_heads, head_dim, chunk)
    conv = _long_conv(u, khat, g_pair, g_inv_pair, f_fwd, f_inv)
    return _out_proj(conv, ga, yb, x, w_out.astype(BF16), row(post_norm_w))
```

```python
import functools
import math

import numpy as np
import jax
import jax.numpy as jnp
from jax import lax
from jax.experimental import pallas as pl
from jax.experimental.pallas import tpu as pltpu

F32 = jnp.float32
BF16 = jnp.bfloat16

EPS = 1e-6
DECAY_TARGET = 1e-2
FAST_DECAY_PCT = 0.3
SLOW_DECAY_PCT = 1.5
FILTER_EMB = 33
EMB_PAD = 64

NB = 128
N_FFT = NB * NB
K1_COUNT = NB // 2 + 1
IM_ROW = 72
STAGE_ROWS = 144
ROW_PITCH = 152
CH_TILE = 256
LANE = 128
TOKEN_TILE = 512
OUT_TOKEN_TILE = 1024
HALO = 8
FILTER_GROUP = 16
STAGE_UNROLL = 16
SPECTRUM_UNROLL = 8
VMEM_LIMIT = 60 * 1024 * 1024


@functools.lru_cache(maxsize=None)
def _dft_tables():
    k1 = np.arange(K1_COUNT, dtype=np.int64)
    n1 = np.arange(NB, dtype=np.int64)
    n2 = np.arange(NB, dtype=np.int64)
    phase = (k1[None, :, None] * (NB * n1[None, None, :] + n2[:, None, None])) % N_FFT
    theta = 2.0 * np.pi * phase.astype(np.float64) / N_FFT
    g = np.zeros((NB, STAGE_ROWS, NB), np.float64)
    g[:, :K1_COUNT, :] = np.cos(theta)
    g[:, IM_ROW:IM_ROW + K1_COUNT, :] = -np.sin(theta)
    herm = np.full((K1_COUNT,), 2.0)
    herm[0] = 1.0
    herm[-1] = 1.0
    row_w = np.zeros((STAGE_ROWS,), np.float64)
    row_w[:K1_COUNT] = herm / N_FFT
    row_w[IM_ROW:IM_ROW + K1_COUNT] = herm / N_FFT
    g_inv = np.transpose(g[:, :, :NB // 2] * row_w[None, :, None], (0, 2, 1))
    a = np.arange(NB, dtype=np.int64)
    ang = 2.0 * np.pi * ((a[:, None] * a[None, :]) % NB).astype(np.float64) / NB
    c, s = np.cos(ang), np.sin(ang)
    f_fwd = np.block([[c, s], [-s, c]])
    f_inv = np.block([[c, -s], [s, c]])
    half = NB // 2
    g_pair = np.zeros((half, 2, STAGE_ROWS, half, 2), np.float64)
    g_inv_pair = np.zeros((half, half, 2, 2, STAGE_ROWS), np.float64)
    for p in range(2):
        g_pair[:, p, :, :, p] = g[p::2, :, :half]
        g_inv_pair[:, :, p, p, :] = g_inv[p::2]
    g_pair = g_pair.reshape(half, 2 * STAGE_ROWS, NB)
    g_inv_pair = g_inv_pair.reshape(half, NB, 2 * STAGE_ROWS)
    g_filt = g.copy()
    g_filt[1:, :, half:] = g[1:, :, :half - 1:-1]
    g_filt[0, :, half] = 0.0
    g_filt[0, :, half + 1:] = g[0, :, :half:-1]
    return tuple(t.astype(np.float32) for t in (g_filt, g_pair, g_inv_pair, f_fwd, f_inv))


def _dot(a, b):
    return jnp.dot(a, b, preferred_element_type=F32)


def _dot_split(a, b):
    a_hi = a.astype(BF16)
    b_hi = b.astype(BF16)
    a_lo = (a - a_hi.astype(F32)).astype(BF16)
    b_lo = (b - b_hi.astype(F32)).astype(BF16)
    return _dot(a_hi, b_hi) + _dot(a_lo, b_hi) + _dot(a_hi, b_lo)


def _silu(x):
    return x / (1.0 + jnp.exp(-x))


def _rms_norm(x, w):
    return x * lax.rsqrt(jnp.mean(x * x, axis=-1, keepdims=True) + EPS) * w


def _filter_mlp_kernel(z_ref, ta_ref, tb_ref, delta_ref, w1_ref, b1_ref, f1_ref, w2_ref, b2_ref,
                       f2_ref, w3_ref, b3_ref, f3_ref, wo_ref, kf_ref, kb_ref, l1_ref, *, d_a):
    step = pl.program_id(0)
    n_ct = kf_ref.shape[0]
    quarter = NB // 4
    h = jnp.sin(f1_ref[...] * (_dot_split(z_ref[...], w1_ref[...]) + b1_ref[...]))
    h = jnp.sin(f2_ref[...] * (_dot_split(h, w2_ref[...]) + b2_ref[...]))
    h = jnp.sin(f3_ref[...] * (_dot_split(h, w3_ref[...]) + b3_ref[...])).astype(BF16)

    @pl.when(step == 0)
    def _():
        l1_ref[...] = jnp.zeros_like(l1_ref)

    row = lax.broadcasted_iota(jnp.int32, (h.shape[0], CH_TILE), 0)
    no_tap = jnp.logical_and(row == 0, step == 0)
    for half, t_ref in enumerate((ta_ref, tb_ref)):
        for ct in range(n_ct):
            cols = slice(ct * CH_TILE, (ct + 1) * CH_TILE)
            decay = jnp.exp(-t_ref[...] * delta_ref[:, cols])
            for direction, out_ref in enumerate((kf_ref, kb_ref)):
                col0 = (half * 2 + direction) * d_a + ct * CH_TILE
                val = _dot(h, wo_ref[:, col0:col0 + CH_TILE]) * decay
                if direction == 1 and half == 0:
                    val = jnp.where(no_tap, 0.0, val)
                l1_ref[:, cols] += jnp.sum(jnp.abs(val), axis=0, keepdims=True)
                for g in range(FILTER_GROUP):
                    out_ref[ct, half * quarter:(half + 1) * quarter,
                            g * CH_TILE:(g + 1) * CH_TILE] = (
                                val[g * quarter:(g + 1) * quarter].astype(BF16))


@functools.lru_cache(maxsize=None)
def _filter_position_tables(seq_len, d_a):
    steps = NB // FILTER_GROUP
    quarter = NB // 4
    step, g, q, half = np.meshgrid(np.arange(steps), np.arange(FILTER_GROUP), np.arange(quarter),
                                   np.arange(2), indexing="ij")
    pos = (NB * (half * quarter + q) + step * FILTER_GROUP + g).astype(np.float64)
    bands = (FILTER_EMB - 1) // 2
    t = pos / (seq_len - 1)
    w = 2.0 * math.pi * pos / seq_len
    f = np.linspace(1e-4, bands - 1, bands)
    z = np.zeros(pos.shape + (EMB_PAD,), np.float64)
    z[..., 0] = t
    z[..., 1:1 + bands] = np.cos(f * w[..., None])
    z[..., 1 + bands:1 + 2 * bands] = -np.sin(f * w[..., None])
    rows = steps * FILTER_GROUP * quarter
    deltas = np.abs(np.linspace(math.log(DECAY_TARGET) / SLOW_DECAY_PCT,
                                math.log(DECAY_TARGET) / FAST_DECAY_PCT, d_a))
    t = t.reshape(rows, 2)
    return (z.reshape(rows, 2 * EMB_PAD).astype(np.float32), t[:, 0:1].astype(np.float32),
            t[:, 1:2].astype(np.float32), deltas[None, :].astype(np.float32))


def _filter_mlp(w1, b1, f1, w2, b2, f2, w3, b3, f3, wo, seq_len, d_a):
    n_ct = d_a // CH_TILE
    steps = NB // FILTER_GROUP
    rows = FILTER_GROUP * NB // 4
    width = w2.shape[0]
    z2, ta, tb, deltas = _filter_position_tables(seq_len, d_a)
    const = lambda shape: pl.BlockSpec(shape, lambda i: (0,) * len(shape))
    taps = pl.BlockSpec((n_ct, NB // 2, FILTER_GROUP * CH_TILE), lambda i: (0, 0, i))
    return pl.pallas_call(
        functools.partial(_filter_mlp_kernel, d_a=d_a),
        grid=(steps,),
        in_specs=[
            pl.BlockSpec((rows, width), lambda i: (i, 0)),
            pl.BlockSpec((rows, 1), lambda i: (i, 0)),
            pl.BlockSpec((rows, 1), lambda i: (i, 0)),
            const((1, d_a)),
            const((width, width)), const((1, width)), const((1, width)),
            const((width, width)), const((1, width)), const((1, width)),
            const((width, width)), const((1, width)), const((1, width)),
            const(wo.shape),
        ],
        out_specs=[taps, taps, pl.BlockSpec((1, d_a), lambda i: (0, 0))],
        out_shape=[
            jax.ShapeDtypeStruct((n_ct, NB // 2, NB * CH_TILE), BF16),
            jax.ShapeDtypeStruct((n_ct, NB // 2, NB * CH_TILE), BF16),
            jax.ShapeDtypeStruct((1, d_a), F32),
        ],
        compiler_params=pltpu.CompilerParams(dimension_semantics=("arbitrary",)),
        name="filter_mlp",
    )(z2, ta, tb, deltas, w1, b1, f1, w2, b2, f2, w3, b3, f3, wo)


def _store_slab(s_ref, n2, res):
    row = pl.multiple_of(n2 * ROW_PITCH, 8)
    s_ref[0, pl.ds(row, STAGE_ROWS), :] = res[:, :LANE]
    s_ref[1, pl.ds(row, STAGE_ROWS), :] = res[:, LANE:]


def _load_slab(s_ref, n2):
    row = pl.multiple_of(n2 * ROW_PITCH, 8)
    return jnp.concatenate(
        [s_ref[0, pl.ds(row, STAGE_ROWS), :], s_ref[1, pl.ds(row, STAGE_ROWS), :]], axis=1)


def _stage_one_filter(kf_ref, kb_ref, g_ref, s_ref):
    def body(n2, carry):
        col = pl.multiple_of(n2 * CH_TILE, CH_TILE)
        mirror = pl.multiple_of(((NB - n2) & (NB - 1)) * CH_TILE, CH_TILE)
        src = jnp.concatenate(
            [kf_ref[:, pl.ds(col, CH_TILE)], kb_ref[:, pl.ds(mirror, CH_TILE)]], axis=0)
        _store_slab(s_ref, n2, _dot(g_ref[n2], src))
        return carry

    lax.fori_loop(0, NB, body, 0, unroll=STAGE_UNROLL)


def _pack_rows(x):
    return pltpu.bitcast(x.astype(BF16), jnp.uint32)


def _stage_one_packed(u_ref, g_ref, s_ref):
    def body(m, carry):
        w = jnp.concatenate(
            [u_ref.at[half][pl.ds(m, NB // 2, stride=NB // 2), :] for half in range(2)], axis=1)
        res = _dot(g_ref[m], pltpu.bitcast(w, BF16))
        _store_slab(s_ref, 2 * m, res[:STAGE_ROWS])
        _store_slab(s_ref, 2 * m + 1, res[STAGE_ROWS:])
        return carry

    lax.fori_loop(0, NB // 2, body, 0, unroll=STAGE_UNROLL // 2)


def _load_k1(s_ref, k1):
    parts = []
    for off in (0, IM_ROW):
        parts.append(jnp.concatenate(
            [s_ref.at[half][pl.ds(k1 + off, NB, stride=ROW_PITCH), :] for half in range(2)],
            axis=1))
    return jnp.concatenate(parts, axis=0)


def _store_k1(s_ref, k1, val):
    for i, off in enumerate((0, IM_ROW)):
        for half in range(2):
            s_ref.at[half][pl.ds(k1 + off, NB, stride=ROW_PITCH), :] = (
                val[i * NB:(i + 1) * NB, half * LANE:(half + 1) * LANE])


def _filter_fft_kernel(kf_ref, kb_ref, g_ref, f_ref, l1_ref, skip_ref, khat_ref, s_ref):
    _stage_one_filter(kf_ref, kb_ref, g_ref, s_ref)
    scale = 1.0 / (l1_ref[...] + EPS)
    is_real = lax.broadcasted_iota(jnp.int32, (2 * NB, CH_TILE), 0) < NB
    shift = jnp.where(is_real, skip_ref[...], 0.0)

    def body(k1, carry):
        x = _dot(f_ref[...], _load_k1(s_ref, k1).astype(BF16))
        khat_ref[k1] = (x * scale + shift).astype(BF16)
        return carry

    lax.fori_loop(0, K1_COUNT, body, 0, unroll=SPECTRUM_UNROLL)


def _filter_fft(kf, kb, g_filt, f_fwd, l1, skip):
    n_ct = kf.shape[0]
    taps = pl.BlockSpec((None, NB // 2, NB * CH_TILE), lambda c: (c, 0, 0))
    return pl.pallas_call(
        _filter_fft_kernel,
        grid=(n_ct,),
        in_specs=[
            taps, taps,
            pl.BlockSpec((NB, STAGE_ROWS, NB), lambda c: (0, 0, 0),
                         pipeline_mode=pl.Buffered(1)),
            pl.BlockSpec((2 * NB, 2 * NB), lambda c: (0, 0)),
            pl.BlockSpec((1, CH_TILE), lambda c: (0, c)),
            pl.BlockSpec((1, CH_TILE), lambda c: (0, c)),
        ],
        out_specs=pl.BlockSpec((None, K1_COUNT, 2 * NB, CH_TILE), lambda c: (c, 0, 0, 0)),
        out_shape=jax.ShapeDtypeStruct((n_ct, K1_COUNT, 2 * NB, CH_TILE), BF16),
        scratch_shapes=[pltpu.VMEM((2, NB * ROW_PITCH, LANE), F32)],
        compiler_params=pltpu.CompilerParams(
            dimension_semantics=("arbitrary",), vmem_limit_bytes=VMEM_LIMIT),
        name="filter_fft",
    )(kf, kb, g_filt, f_fwd, l1, skip)


def _long_conv_kernel(u_ref, khat_ref, g_ref, ginv_ref, f_ref, finv_ref, o_ref, s_ref):
    _stage_one_packed(u_ref, g_ref, s_ref)

    def spectrum_body(k1, carry):
        x = _dot(f_ref[...], _load_k1(s_ref, k1).astype(BF16)).astype(BF16)
        kh = khat_ref[k1]
        xr, xi = x[:NB], x[NB:]
        kr, ki = kh[:NB], kh[NB:]
        y = jnp.concatenate([xr * kr - xi * ki, xr * ki + xi * kr], axis=0)
        _store_k1(s_ref, k1, _dot(finv_ref[...], y))
        return carry

    lax.fori_loop(0, K1_COUNT, spectrum_body, 0, unroll=SPECTRUM_UNROLL)

    def inverse_body(m, carry):
        b = jnp.concatenate([_load_slab(s_ref, 2 * m), _load_slab(s_ref, 2 * m + 1)], axis=0)
        w = _pack_rows(_dot(ginv_ref[m], b.astype(BF16)))
        for half in range(2):
            o_ref.at[half][pl.ds(m, NB // 2, stride=NB // 2), :] = (
                w[:, half * LANE:(half + 1) * LANE])
        return carry

    lax.fori_loop(0, NB // 2, inverse_body, 0, unroll=STAGE_UNROLL // 2)


def _long_conv(u, khat, g_pair, g_inv_pair, f_fwd, f_inv):
    batch, n_slab, half_len, _ = u.shape
    slabs = CH_TILE // LANE
    n_ct = n_slab // slabs
    const = lambda shape: pl.BlockSpec(shape, lambda c, b: (0,) * len(shape),
                                       pipeline_mode=pl.Buffered(1))
    return pl.pallas_call(
        _long_conv_kernel,
        grid=(n_ct, batch),
        in_specs=[
            pl.BlockSpec((None, slabs, half_len, LANE), lambda c, b: (b, c, 0, 0)),
            pl.BlockSpec((None, K1_COUNT, 2 * NB, CH_TILE), lambda c, b: (c, 0, 0, 0),
                         pipeline_mode=pl.Buffered(1)),
            const(g_pair.shape),
            const(g_inv_pair.shape),
            const((2 * NB, 2 * NB)),
            const((2 * NB, 2 * NB)),
        ],
        out_specs=pl.BlockSpec((None, slabs, half_len, LANE), lambda c, b: (b, c, 0, 0)),
        out_shape=jax.ShapeDtypeStruct(u.shape, jnp.uint32),
        scratch_shapes=[pltpu.VMEM((2, NB * ROW_PITCH, LANE), F32)],
        compiler_params=pltpu.CompilerParams(
            dimension_semantics=("arbitrary", "arbitrary"), vmem_limit_bytes=VMEM_LIMIT),
        name="long_conv",
    )(u, khat, g_pair, g_inv_pair, f_fwd, f_inv)


def _in_proj_kernel(x_ref, xp_ref, xn_ref, pnw_ref, w_ref, cw_ref, cb_ref,
                    lnw_ref, lnb_ref, sw_ref, sb_ref, u_ref, ga_ref, yb_ref,
                    *, n_ct, n_heads, head_dim, chunk):
    i = pl.program_id(1)
    last = pl.num_programs(1) - 1
    tm = x_ref.shape[0]
    pnw = pnw_ref[...]
    h_main = _rms_norm(x_ref[...], pnw)
    h_prev = jnp.where(i == 0, 0.0, _rms_norm(xp_ref[...], pnw))
    h_next = jnp.where(i == last, 0.0, _rms_norm(xn_ref[...], pnw))
    h_ext = jnp.concatenate([h_prev, h_main, h_next], axis=0).astype(BF16)
    h_bf = h_main.astype(BF16)
    rows = tm + 2 * HALO

    d_a = n_ct * CH_TILE
    for ct in range(n_ct):
        cols = slice(ct * CH_TILE, (ct + 1) * CH_TILE)
        conv = []
        for which in range(3):
            p = _dot(h_ext, w_ref[:, which * d_a + ct * CH_TILE:which * d_a + (ct + 1) * CH_TILE])
            cw = cw_ref[which * 3:(which + 1) * 3, cols]
            conv.append(pltpu.roll(p, 1, axis=0)[HALO:HALO + tm] * cw[0:1]
                        + p[HALO:HALO + tm] * cw[1:2]
                        + pltpu.roll(p, rows - 1, axis=0)[HALO:HALO + tm] * cw[2:3]
                        + cb_ref[which:which + 1, cols])
        gate = _dot(h_bf, w_ref[:, 3 * d_a + ct * CH_TILE:3 * d_a + (ct + 1) * CH_TILE])
        u = conv[2] * conv[1]
        slabs = CH_TILE // LANE
        for half in range(slabs):
            u_ref[ct * slabs + half] = _pack_rows(u[:, half * LANE:(half + 1) * LANE])
        ga_ref[:, ct * CH_TILE:(ct + 1) * CH_TILE] = (conv[0] * _silu(gate)).astype(BF16)

    d_b = n_heads * head_dim
    pair = 2 * head_dim
    n_chunks = tm // chunk
    for hp in range(n_heads // 2):
        q = [_dot(h_bf, w_ref[:, 4 * d_a + which * d_b + hp * pair:
                              4 * d_a + which * d_b + (hp + 1) * pair]) for which in range(3)]
        for sub in range(2):
            hd = hp * 2 + sub
            lo = sub * head_dim
            su = q[0][:, lo:lo + head_dim]
            sv = q[1][:, lo:lo + head_dim]
            sg = q[2][:, lo:lo + head_dim]
            mu = jnp.mean(sv, axis=-1, keepdims=True)
            d = sv - mu
            var = jnp.mean(d * d, axis=-1, keepdims=True)
            vn = (d * lax.rsqrt(var + EPS) * lnw_ref[:, hd * head_dim:(hd + 1) * head_dim]
                  + lnb_ref[:, hd * head_dim:(hd + 1) * head_dim]).astype(BF16)
            wide = jnp.concatenate(
                [vn[c * chunk:(c + 1) * chunk] for c in range(n_chunks)], axis=1)
            mixed = _dot(sw_ref[hd], wide) + sb_ref[hd]
            mixed = jnp.concatenate(
                [mixed[:, c * head_dim:(c + 1) * head_dim] for c in range(n_chunks)], axis=0)
            yb_ref[:, hd * head_dim:(hd + 1) * head_dim] = (su * mixed * _silu(sg)).astype(BF16)


def _in_proj(x, pre_norm_w, w, cw, cb, lnw, lnb, sw, sb, n_ct, n_heads, head_dim, chunk):
    batch, seq_len, d_model = x.shape
    tm = TOKEN_TILE
    steps = seq_len // tm
    halo_blocks = seq_len // HALO
    per_tile = tm // HALO
    const = lambda shape: pl.BlockSpec(shape, lambda b, i: (0,) * len(shape))
    const1 = lambda shape: pl.BlockSpec(shape, lambda b, i: (0,) * len(shape),
                                        pipeline_mode=pl.Buffered(1))
    d_b = n_heads * head_dim
    d_a = n_ct * CH_TILE
    return pl.pallas_call(
        functools.partial(_in_proj_kernel, n_ct=n_ct, n_heads=n_heads, head_dim=head_dim,
                          chunk=chunk),
        grid=(batch, steps),
        in_specs=[
            pl.BlockSpec((None, tm, d_model), lambda b, i: (b, i, 0)),
            pl.BlockSpec((None, HALO, d_model),
                         lambda b, i: (b, jnp.maximum(i * per_tile - 1, 0), 0)),
            pl.BlockSpec((None, HALO, d_model),
                         lambda b, i: (b, jnp.minimum((i + 1) * per_tile, halo_blocks - 1), 0)),
            const((1, d_model)),
            const1(w.shape),
            const(cw.shape), const(cb.shape), const(lnw.shape), const(lnb.shape),
            const(sw.shape), const(sb.shape),
        ],
        out_specs=[
            pl.BlockSpec((None, d_a // LANE, tm // 2, LANE), lambda b, i: (b, 0, i, 0)),
            pl.BlockSpec((None, tm, d_a), lambda b, i: (b, i, 0)),
            pl.BlockSpec((None, tm, d_b), lambda b, i: (b, i, 0)),
        ],
        out_shape=[
            jax.ShapeDtypeStruct((batch, d_a // LANE, seq_len // 2, LANE), jnp.uint32),
            jax.ShapeDtypeStruct((batch, seq_len, d_a), BF16),
            jax.ShapeDtypeStruct((batch, seq_len, d_b), BF16),
        ],
        compiler_params=pltpu.CompilerParams(
            dimension_semantics=("arbitrary", "arbitrary"), vmem_limit_bytes=VMEM_LIMIT),
        name="in_proj",
    )(x, x, x, pre_norm_w, w, cw, cb, lnw, lnb, sw, sb)


def _out_proj_kernel(c_ref, ga_ref, yb_ref, x_ref, w_ref, pw_ref, o_ref):
    conv = jnp.concatenate(
        [pltpu.bitcast(c_ref[s], BF16) for s in range(c_ref.shape[0])], axis=1)
    ya = (conv.astype(F32) * ga_ref[...].astype(F32)).astype(BF16)
    yc = jnp.concatenate([ya, yb_ref[...]], axis=1)
    y = _dot(yc, w_ref[...])
    o_ref[...] = x_ref[...] + _rms_norm(y, pw_ref[...])


def _out_proj(conv, ga, yb, x, w_out, post_norm_w):
    batch, seq_len, d_model = x.shape
    n_slab = conv.shape[1]
    tm = OUT_TOKEN_TILE
    return pl.pallas_call(
        _out_proj_kernel,
        grid=(batch, seq_len // tm),
        in_specs=[
            pl.BlockSpec((None, n_slab, tm // 2, LANE), lambda b, i: (b, 0, i, 0)),
            pl.BlockSpec((None, tm, ga.shape[-1]), lambda b, i: (b, i, 0)),
            pl.BlockSpec((None, tm, yb.shape[-1]), lambda b, i: (b, i, 0)),
            pl.BlockSpec((None, tm, d_model), lambda b, i: (b, i, 0)),
            pl.BlockSpec(w_out.shape, lambda b, i: (0, 0)),
            pl.BlockSpec((1, d_model), lambda b, i: (0, 0)),
        ],
        out_specs=pl.BlockSpec((None, tm, d_model), lambda b, i: (b, i, 0)),
        out_shape=jax.ShapeDtypeStruct(x.shape, x.dtype),
        compiler_params=pltpu.CompilerParams(
            dimension_semantics=("arbitrary", "arbitrary"), vmem_limit_bytes=VMEM_LIMIT),
        name="out_proj",
    )(conv, ga, yb, x, w_out, post_norm_w)


def kernel(x, pre_norm_w, w_in, conv_w, conv_b, filt_w1, filt_b1, filt_freq1, filt_w2, filt_b2, filt_freq2, filt_w3, filt_b3, filt_freq3, filt_w_out, hyena_skip, sgu_norm_w, sgu_norm_b, sgu_w, sgu_b, w_out, post_norm_w):
    batch, seq_len, d_model = x.shape
    d_a = hyena_skip.shape[0]
    n_heads, chunk, _ = sgu_w.shape
    d_b = sgu_norm_w.shape[0]
    head_dim = d_b // n_heads
    n_ct = d_a // CH_TILE
    assert 2 * seq_len == N_FFT and d_a % CH_TILE == 0 and head_dim == LANE
    assert seq_len % TOKEN_TILE == 0 and TOKEN_TILE % chunk == 0 and n_heads % 2 == 0
    assert seq_len % OUT_TOKEN_TILE == 0
    assert w_in.shape[1] == 4 * d_a + 3 * d_b

    g_filt, g_pair, g_inv_pair, f_fwd, f_inv = (
        jnp.asarray(t).astype(BF16) for t in _dft_tables())

    row = lambda v: v.astype(F32).reshape(1, -1)
    twice = lambda w: jnp.kron(jnp.eye(2, dtype=F32), w.astype(F32))
    tiled = lambda v: jnp.tile(row(v), (1, 2))
    assert filt_w2.shape[0] == EMB_PAD
    w1 = jnp.pad(filt_w1.astype(F32), ((0, EMB_PAD - FILTER_EMB), (0, 0)))
    kf, kb, l1 = _filter_mlp(twice(w1), tiled(filt_b1), tiled(filt_freq1),
                             twice(filt_w2), tiled(filt_b2), tiled(filt_freq2),
                             twice(filt_w3), tiled(filt_b3), tiled(filt_freq3),
                             twice(filt_w_out).astype(BF16), seq_len, d_a)
    khat = _filter_fft(kf, kb, g_filt, f_fwd, l1, row(hyena_skip))

    cw = jnp.transpose(conv_w.astype(F32).reshape(3, 3, d_a), (1, 0, 2)).reshape(9, d_a)
    cb = conv_b.astype(F32).reshape(3, d_a)
    sb = jnp.broadcast_to(sgu_b.astype(F32)[:, :, None], (n_heads, chunk, 1))

    u, ga, yb = _in_proj(x, row(pre_norm_w), w_in.astype(BF16), cw, cb, row(sgu_norm_w),
                         row(sgu_norm_b), sgu_w.astype(BF16), sb,
                         n_ct, n_heads, head_dim, chunk)
    conv = _long_conv(u, khat, g_pair, g_inv_pair, f_fwd, f_inv)
    return _out_proj(conv, ga, yb, x, w_out.astype(BF16), row(post_norm_w))
```

```python
import functools
import math

import numpy as np
import jax
import jax.numpy as jnp
from jax import lax
from jax.experimental import pallas as pl
from jax.experimental.pallas import tpu as pltpu

F32 = jnp.float32
BF16 = jnp.bfloat16

EPS = 1e-6
DECAY_TARGET = 1e-2
FAST_DECAY_PCT = 0.3
SLOW_DECAY_PCT = 1.5
FILTER_EMB = 33
EMB_PAD = 64

NB = 128
N_FFT = NB * NB
K1_COUNT = NB // 2 + 1
IM_ROW = 72
STAGE_ROWS = 144
ROW_PITCH = 152
CH_TILE = 256
LANE = 128
TOKEN_TILE = 1024
OUT_TOKEN_TILE = 1024
HALO = 8
FILTER_GROUP = 16
STAGE_UNROLL = 16
SPECTRUM_UNROLL = 13
VMEM_LIMIT = 60 * 1024 * 1024


@functools.lru_cache(maxsize=None)
def _dft_tables():
    k1 = np.arange(K1_COUNT, dtype=np.int64)
    n1 = np.arange(NB, dtype=np.int64)
    n2 = np.arange(NB, dtype=np.int64)
    phase = (k1[None, :, None] * (NB * n1[None, None, :] + n2[:, None, None])) % N_FFT
    theta = 2.0 * np.pi * phase.astype(np.float64) / N_FFT
    g = np.zeros((NB, STAGE_ROWS, NB), np.float64)
    g[:, :K1_COUNT, :] = np.cos(theta)
    g[:, IM_ROW:IM_ROW + K1_COUNT, :] = -np.sin(theta)
    herm = np.full((K1_COUNT,), 2.0)
    herm[0] = 1.0
    herm[-1] = 1.0
    row_w = np.zeros((STAGE_ROWS,), np.float64)
    row_w[:K1_COUNT] = herm / N_FFT
    row_w[IM_ROW:IM_ROW + K1_COUNT] = herm / N_FFT
    g_inv = np.transpose(g[:, :, :NB // 2] * row_w[None, :, None], (0, 2, 1))
    a = np.arange(NB, dtype=np.int64)
    ang = 2.0 * np.pi * ((a[:, None] * a[None, :]) % NB).astype(np.float64) / NB
    c, s = np.cos(ang), np.sin(ang)
    f_fwd = np.block([[c, s], [-s, c]])
    f_inv = np.block([[c, -s], [s, c]])
    half = NB // 2
    g_pair = np.zeros((half, 2, STAGE_ROWS, half, 2), np.float64)
    g_inv_pair = np.zeros((half, half, 2, 2, STAGE_ROWS), np.float64)
    for p in range(2):
        g_pair[:, p, :, :, p] = g[p::2, :, :half]
        g_inv_pair[:, :, p, p, :] = g_inv[p::2]
    g_pair = g_pair.reshape(half, 2 * STAGE_ROWS, NB)
    g_inv_pair = g_inv_pair.reshape(half, NB, 2 * STAGE_ROWS)
    g_filt = g.copy()
    g_filt[1:, :, half:] = g[1:, :, :half - 1:-1]
    g_filt[0, :, half] = 0.0
    g_filt[0, :, half + 1:] = g[0, :, :half:-1]
    return tuple(t.astype(np.float32) for t in (g_filt, g_pair, g_inv_pair, f_fwd, f_inv))


def _dot(a, b):
    return jnp.dot(a, b, preferred_element_type=F32)


def _dot_split(a, b):
    a_hi = a.astype(BF16)
    b_hi = b.astype(BF16)
    a_lo = (a - a_hi.astype(F32)).astype(BF16)
    b_lo = (b - b_hi.astype(F32)).astype(BF16)
    return _dot(a_hi, b_hi) + _dot(a_lo, b_hi) + _dot(a_hi, b_lo)


def _silu(x):
    return x / (1.0 + jnp.exp(-x))


def _rms_norm(x, w):
    return x * lax.rsqrt(jnp.mean(x * x, axis=-1, keepdims=True) + EPS) * w


def _filter_mlp_kernel(z_ref, ta_ref, tb_ref, delta_ref, w1_ref, b1_ref, f1_ref, w2_ref, b2_ref,
                       f2_ref, w3_ref, b3_ref, f3_ref, wo_ref, kf_ref, kb_ref, l1_ref, *, d_a):
    step = pl.program_id(0)
    n_ct = kf_ref.shape[0]
    quarter = NB // 4
    h = jnp.sin(f1_ref[...] * (_dot_split(z_ref[...], w1_ref[...]) + b1_ref[...]))
    h = jnp.sin(f2_ref[...] * (_dot_split(h, w2_ref[...]) + b2_ref[...]))
    h = jnp.sin(f3_ref[...] * (_dot_split(h, w3_ref[...]) + b3_ref[...])).astype(BF16)

    @pl.when(step == 0)
    def _():
        l1_ref[...] = jnp.zeros_like(l1_ref)

    row = lax.broadcasted_iota(jnp.int32, (h.shape[0], CH_TILE), 0)
    no_tap = jnp.logical_and(row == 0, step == 0)
    for half, t_ref in enumerate((ta_ref, tb_ref)):
        for ct in range(n_ct):
            cols = slice(ct * CH_TILE, (ct + 1) * CH_TILE)
            decay = jnp.exp(-t_ref[...] * delta_ref[:, cols])
            for direction, out_ref in enumerate((kf_ref, kb_ref)):
                col0 = (half * 2 + direction) * d_a + ct * CH_TILE
                val = _dot(h, wo_ref[:, col0:col0 + CH_TILE]) * decay
                if direction == 1 and half == 0:
                    val = jnp.where(no_tap, 0.0, val)
                l1_ref[:, cols] += jnp.sum(jnp.abs(val), axis=0, keepdims=True)
                for g in range(FILTER_GROUP):
                    out_ref[ct, half * quarter:(half + 1) * quarter,
                            g * CH_TILE:(g + 1) * CH_TILE] = (
                                val[g * quarter:(g + 1) * quarter].astype(BF16))


@functools.lru_cache(maxsize=None)
def _filter_position_tables(seq_len, d_a):
    steps = NB // FILTER_GROUP
    quarter = NB // 4
    step, g, q, half = np.meshgrid(np.arange(steps), np.arange(FILTER_GROUP), np.arange(quarter),
                                   np.arange(2), indexing="ij")
    pos = (NB * (half * quarter + q) + step * FILTER_GROUP + g).astype(np.float64)
    bands = (FILTER_EMB - 1) // 2
    t = pos / (seq_len - 1)
    w = 2.0 * math.pi * pos / seq_len
    f = np.linspace(1e-4, bands - 1, bands)
    z = np.zeros(pos.shape + (EMB_PAD,), np.float64)
    z[..., 0] = t
    z[..., 1:1 + bands] = np.cos(f * w[..., None])
    z[..., 1 + bands:1 + 2 * bands] = -np.sin(f * w[..., None])
    rows = steps * FILTER_GROUP * quarter
    deltas = np.abs(np.linspace(math.log(DECAY_TARGET) / SLOW_DECAY_PCT,
                                math.log(DECAY_TARGET) / FAST_DECAY_PCT, d_a))
    t = t.reshape(rows, 2)
    return (z.reshape(rows, 2 * EMB_PAD).astype(np.float32), t[:, 0:1].astype(np.float32),
            t[:, 1:2].astype(np.float32), deltas[None, :].astype(np.float32))


def _filter_mlp(w1, b1, f1, w2, b2, f2, w3, b3, f3, wo, seq_len, d_a):
    n_ct = d_a // CH_TILE
    steps = NB // FILTER_GROUP
    rows = FILTER_GROUP * NB // 4
    width = w2.shape[0]
    z2, ta, tb, deltas = _filter_position_tables(seq_len, d_a)
    const = lambda shape: pl.BlockSpec(shape, lambda i: (0,) * len(shape))
    taps = pl.BlockSpec((n_ct, NB // 2, FILTER_GROUP * CH_TILE), lambda i: (0, 0, i))
    return pl.pallas_call(
        functools.partial(_filter_mlp_kernel, d_a=d_a),
        grid=(steps,),
        in_specs=[
            pl.BlockSpec((rows, width), lambda i: (i, 0)),
            pl.BlockSpec((rows, 1), lambda i: (i, 0)),
            pl.BlockSpec((rows, 1), lambda i: (i, 0)),
            const((1, d_a)),
            const((width, width)), const((1, width)), const((1, width)),
            const((width, width)), const((1, width)), const((1, width)),
            const((width, width)), const((1, width)), const((1, width)),
            const(wo.shape),
        ],
        out_specs=[taps, taps, pl.BlockSpec((1, d_a), lambda i: (0, 0))],
        out_shape=[
            jax.ShapeDtypeStruct((n_ct, NB // 2, NB * CH_TILE), BF16),
            jax.ShapeDtypeStruct((n_ct, NB // 2, NB * CH_TILE), BF16),
            jax.ShapeDtypeStruct((1, d_a), F32),
        ],
        compiler_params=pltpu.CompilerParams(dimension_semantics=("arbitrary",)),
        name="filter_mlp",
    )(z2, ta, tb, deltas, w1, b1, f1, w2, b2, f2, w3, b3, f3, wo)


def _store_slab(s_ref, n2, res):
    row = pl.multiple_of(n2 * ROW_PITCH, 8)
    s_ref[0, pl.ds(row, STAGE_ROWS), :] = res[:, :LANE]
    s_ref[1, pl.ds(row, STAGE_ROWS), :] = res[:, LANE:]


def _load_slab(s_ref, n2):
    row = pl.multiple_of(n2 * ROW_PITCH, 8)
    return jnp.concatenate(
        [s_ref[0, pl.ds(row, STAGE_ROWS), :], s_ref[1, pl.ds(row, STAGE_ROWS), :]], axis=1)


def _stage_one_filter(kf_ref, kb_ref, g_ref, s_ref):
    def body(n2, carry):
        col = pl.multiple_of(n2 * CH_TILE, CH_TILE)
        mirror = pl.multiple_of(((NB - n2) & (NB - 1)) * CH_TILE, CH_TILE)
        src = jnp.concatenate(
            [kf_ref[:, pl.ds(col, CH_TILE)], kb_ref[:, pl.ds(mirror, CH_TILE)]], axis=0)
        _store_slab(s_ref, n2, _dot(g_ref[n2], src))
        return carry

    lax.fori_loop(0, NB, body, 0, unroll=STAGE_UNROLL)


def _pack_rows(x):
    return pltpu.bitcast(x.astype(BF16), jnp.uint32)


def _stage_one_packed(u_ref, g_ref, s_ref):
    def body(m, carry):
        w = jnp.concatenate(
            [u_ref.at[half][pl.ds(m, NB // 2, stride=NB // 2), :] for half in range(2)], axis=1)
        res = _dot(g_ref[m], pltpu.bitcast(w, BF16))
        _store_slab(s_ref, 2 * m, res[:STAGE_ROWS])
        _store_slab(s_ref, 2 * m + 1, res[STAGE_ROWS:])
        return carry

    lax.fori_loop(0, NB // 2, body, 0, unroll=STAGE_UNROLL // 2)


def _load_k1(s_ref, k1):
    parts = []
    for off in (0, IM_ROW):
        parts.append(jnp.concatenate(
            [s_ref.at[half][pl.ds(k1 + off, NB, stride=ROW_PITCH), :] for half in range(2)],
            axis=1))
    return jnp.concatenate(parts, axis=0)


def _store_k1(s_ref, k1, val):
    for i, off in enumerate((0, IM_ROW)):
        for half in range(2):
            s_ref.at[half][pl.ds(k1 + off, NB, stride=ROW_PITCH), :] = (
                val[i * NB:(i + 1) * NB, half * LANE:(half + 1) * LANE])


def _filter_fft_kernel(kf_ref, kb_ref, g_ref, f_ref, l1_ref, skip_ref, khat_ref, s_ref):
    _stage_one_filter(kf_ref, kb_ref, g_ref, s_ref)
    scale = 1.0 / (l1_ref[...] + EPS)
    is_real = lax.broadcasted_iota(jnp.int32, (2 * NB, CH_TILE), 0) < NB
    shift = jnp.where(is_real, skip_ref[...], 0.0)

    def body(k1, carry):
        x = _dot(f_ref[...], _load_k1(s_ref, k1).astype(BF16))
        khat_ref[k1] = (x * scale + shift).astype(BF16)
        return carry

    lax.fori_loop(0, K1_COUNT, body, 0, unroll=SPECTRUM_UNROLL)


def _filter_fft(kf, kb, g_filt, f_fwd, l1, skip):
    n_ct = kf.shape[0]
    taps = pl.BlockSpec((None, NB // 2, NB * CH_TILE), lambda c: (c, 0, 0))
    return pl.pallas_call(
        _filter_fft_kernel,
        grid=(n_ct,),
        in_specs=[
            taps, taps,
            pl.BlockSpec((NB, STAGE_ROWS, NB), lambda c: (0, 0, 0),
                         pipeline_mode=pl.Buffered(1)),
            pl.BlockSpec((2 * NB, 2 * NB), lambda c: (0, 0)),
            pl.BlockSpec((1, CH_TILE), lambda c: (0, c)),
            pl.BlockSpec((1, CH_TILE), lambda c: (0, c)),
        ],
        out_specs=pl.BlockSpec((None, K1_COUNT, 2 * NB, CH_TILE), lambda c: (c, 0, 0, 0)),
        out_shape=jax.ShapeDtypeStruct((n_ct, K1_COUNT, 2 * NB, CH_TILE), BF16),
        scratch_shapes=[pltpu.VMEM((2, NB * ROW_PITCH, LANE), F32)],
        compiler_params=pltpu.CompilerParams(
            dimension_semantics=("arbitrary",), vmem_limit_bytes=VMEM_LIMIT),
        name="filter_fft",
    )(kf, kb, g_filt, f_fwd, l1, skip)


def _long_conv_kernel(u_ref, khat_ref, g_ref, ginv_ref, f_ref, finv_ref, o_ref, s_ref):
    _stage_one_packed(u_ref, g_ref, s_ref)

    def spectrum_body(k1, carry):
        x = _dot(f_ref[...], _load_k1(s_ref, k1).astype(BF16)).astype(BF16)
        kh = khat_ref[k1]
        xr, xi = x[:NB], x[NB:]
        kr, ki = kh[:NB], kh[NB:]
        y = jnp.concatenate([xr * kr - xi * ki, xr * ki + xi * kr], axis=0)
        _store_k1(s_ref, k1, _dot(finv_ref[...], y))
        return carry

    lax.fori_loop(0, K1_COUNT, spectrum_body, 0, unroll=SPECTRUM_UNROLL)

    def inverse_body(m, carry):
        b = jnp.concatenate([_load_slab(s_ref, 2 * m), _load_slab(s_ref, 2 * m + 1)], axis=0)
        w = _pack_rows(_dot(ginv_ref[m], b.astype(BF16)))
        for half in range(2):
            o_ref.at[half][pl.ds(m, NB // 2, stride=NB // 2), :] = (
                w[:, half * LANE:(half + 1) * LANE])
        return carry

    lax.fori_loop(0, NB // 2, inverse_body, 0, unroll=STAGE_UNROLL // 2)


def _long_conv(u, khat, g_pair, g_inv_pair, f_fwd, f_inv):
    batch, n_slab, half_len, _ = u.shape
    slabs = CH_TILE // LANE
    n_ct = n_slab // slabs
    const = lambda shape: pl.BlockSpec(shape, lambda c, b: (0,) * len(shape),
                                       pipeline_mode=pl.Buffered(1))
    return pl.pallas_call(
        _long_conv_kernel,
        grid=(n_ct, batch),
        in_specs=[
            pl.BlockSpec((None, slabs, half_len, LANE), lambda c, b: (b, c, 0, 0)),
            pl.BlockSpec((None, K1_COUNT, 2 * NB, CH_TILE), lambda c, b: (c, 0, 0, 0),
                         pipeline_mode=pl.Buffered(1)),
            const(g_pair.shape),
            const(g_inv_pair.shape),
            const((2 * NB, 2 * NB)),
            const((2 * NB, 2 * NB)),
        ],
        out_specs=pl.BlockSpec((None, slabs, half_len, LANE), lambda c, b: (b, c, 0, 0)),
        out_shape=jax.ShapeDtypeStruct(u.shape, jnp.uint32),
        scratch_shapes=[pltpu.VMEM((2, NB * ROW_PITCH, LANE), F32)],
        compiler_params=pltpu.CompilerParams(
            dimension_semantics=("arbitrary", "arbitrary"), vmem_limit_bytes=VMEM_LIMIT),
        name="long_conv",
    )(u, khat, g_pair, g_inv_pair, f_fwd, f_inv)


def _in_proj_kernel(x_ref, xp_ref, xn_ref, pnw_ref, w_ref, cw_ref, cb_ref,
                    lnw_ref, lnb_ref, sw_ref, sb_ref, u_ref, ga_ref, yb_ref,
                    *, n_ct, n_heads, head_dim, chunk):
    i = pl.program_id(1)
    last = pl.num_programs(1) - 1
    tm = x_ref.shape[0]
    pnw = pnw_ref[...]
    h_main = _rms_norm(x_ref[...], pnw)
    h_prev = jnp.where(i == 0, 0.0, _rms_norm(xp_ref[...], pnw))
    h_next = jnp.where(i == last, 0.0, _rms_norm(xn_ref[...], pnw))
    h_ext = jnp.concatenate([h_prev, h_main, h_next], axis=0).astype(BF16)
    h_bf = h_main.astype(BF16)
    rows = tm + 2 * HALO

    d_a = n_ct * CH_TILE
    for ct in range(n_ct):
        cols = slice(ct * CH_TILE, (ct + 1) * CH_TILE)
        conv = []
        for which in range(3):
            p = _dot(h_ext, w_ref[:, which * d_a + ct * CH_TILE:which * d_a + (ct + 1) * CH_TILE])
            cw = cw_ref[which * 3:(which + 1) * 3, cols]
            conv.append(pltpu.roll(p, 1, axis=0)[HALO:HALO + tm] * cw[0:1]
                        + p[HALO:HALO + tm] * cw[1:2]
                        + pltpu.roll(p, rows - 1, axis=0)[HALO:HALO + tm] * cw[2:3]
                        + cb_ref[which:which + 1, cols])
        gate = _dot(h_bf,
                    w_ref[:, 3 * d_a + ct * CH_TILE:3 * d_a + (ct + 1) * CH_TILE])
        u = conv[2] * conv[1]
        slabs = CH_TILE // LANE
        for half in range(slabs):
            u_ref[ct * slabs + half] = _pack_rows(u[:, half * LANE:(half + 1) * LANE])
        ga_ref[:, ct * CH_TILE:(ct + 1) * CH_TILE] = (conv[0] * _silu(gate)).astype(BF16)

    d_b = n_heads * head_dim
    pair = 2 * head_dim
    n_chunks = tm // chunk
    for hp in range(n_heads // 2):
        q = [_dot(h_bf, w_ref[:, 4 * d_a + which * d_b + hp * pair:
                                         4 * d_a + which * d_b + (hp + 1) * pair])
             for which in range(3)]
        for sub in range(2):
            hd = hp * 2 + sub
            lo = sub * head_dim
            su = q[0][:, lo:lo + head_dim]
            sv = q[1][:, lo:lo + head_dim]
            sg = q[2][:, lo:lo + head_dim]
            mu = jnp.mean(sv, axis=-1, keepdims=True)
            d = sv - mu
            var = jnp.mean(d * d, axis=-1, keepdims=True)
            vn = (d * lax.rsqrt(var + EPS) * lnw_ref[:, hd * head_dim:(hd + 1) * head_dim]
                  + lnb_ref[:, hd * head_dim:(hd + 1) * head_dim]).astype(BF16)
            wide = jnp.concatenate(
                [vn[c * chunk:(c + 1) * chunk] for c in range(n_chunks)], axis=1)
            mixed = _dot(sw_ref[hd], wide) + sb_ref[hd]
            mixed = jnp.concatenate(
                [mixed[:, c * head_dim:(c + 1) * head_dim] for c in range(n_chunks)], axis=0)
            yb_ref[:, hd * head_dim:(hd + 1) * head_dim] = (su * mixed * _silu(sg)).astype(BF16)


def _in_proj(x, pre_norm_w, w, cw, cb, lnw, lnb, sw, sb, n_ct, n_heads, head_dim, chunk):
    batch, seq_len, d_model = x.shape
    tm = TOKEN_TILE
    steps = seq_len // tm
    halo_blocks = seq_len // HALO
    per_tile = tm // HALO
    const = lambda shape: pl.BlockSpec(shape, lambda b, i: (0,) * len(shape))
    const1 = lambda shape: pl.BlockSpec(shape, lambda b, i: (0,) * len(shape),
                                        pipeline_mode=pl.Buffered(1))
    d_b = n_heads * head_dim
    d_a = n_ct * CH_TILE
    return pl.pallas_call(
        functools.partial(_in_proj_kernel, n_ct=n_ct, n_heads=n_heads, head_dim=head_dim,
                          chunk=chunk),
        grid=(batch, steps),
        in_specs=[
            pl.BlockSpec((None, tm, d_model), lambda b, i: (b, i, 0)),
            pl.BlockSpec((None, HALO, d_model),
                         lambda b, i: (b, jnp.maximum(i * per_tile - 1, 0), 0)),
            pl.BlockSpec((None, HALO, d_model),
                         lambda b, i: (b, jnp.minimum((i + 1) * per_tile, halo_blocks - 1), 0)),
            const((1, d_model)),
            const1(w.shape),
            const(cw.shape), const(cb.shape), const(lnw.shape), const(lnb.shape),
            const(sw.shape), const(sb.shape),
        ],
        out_specs=[
            pl.BlockSpec((None, d_a // LANE, tm // 2, LANE), lambda b, i: (b, 0, i, 0)),
            pl.BlockSpec((None, tm, d_a), lambda b, i: (b, i, 0)),
            pl.BlockSpec((None, tm, d_b), lambda b, i: (b, i, 0)),
        ],
        out_shape=[
            jax.ShapeDtypeStruct((batch, d_a // LANE, seq_len // 2, LANE), jnp.uint32),
            jax.ShapeDtypeStruct((batch, seq_len, d_a), BF16),
            jax.ShapeDtypeStruct((batch, seq_len, d_b), BF16),
        ],
        compiler_params=pltpu.CompilerParams(
            dimension_semantics=("arbitrary", "arbitrary"), vmem_limit_bytes=VMEM_LIMIT),
        name="in_proj",
    )(x, x, x, pre_norm_w, w, cw, cb, lnw, lnb, sw, sb)


def _out_proj_kernel(c_ref, ga_ref, yb_ref, x_ref, w_ref, pw_ref, o_ref):
    conv = jnp.concatenate(
        [pltpu.bitcast(c_ref[s], BF16) for s in range(c_ref.shape[0])], axis=1)
    ya = (conv.astype(F32) * ga_ref[...].astype(F32)).astype(BF16)
    yc = jnp.concatenate([ya, yb_ref[...]], axis=1)
    y = _dot(yc, w_ref[...])
    o_ref[...] = x_ref[...] + _rms_norm(y, pw_ref[...])


def _out_proj(conv, ga, yb, x, w_out, post_norm_w):
    batch, seq_len, d_model = x.shape
    n_slab = conv.shape[1]
    tm = OUT_TOKEN_TILE
    return pl.pallas_call(
        _out_proj_kernel,
        grid=(batch, seq_len // tm),
        in_specs=[
            pl.BlockSpec((None, n_slab, tm // 2, LANE), lambda b, i: (b, 0, i, 0)),
            pl.BlockSpec((None, tm, ga.shape[-1]), lambda b, i: (b, i, 0)),
            pl.BlockSpec((None, tm, yb.shape[-1]), lambda b, i: (b, i, 0)),
            pl.BlockSpec((None, tm, d_model), lambda b, i: (b, i, 0)),
            pl.BlockSpec(w_out.shape, lambda b, i: (0, 0)),
            pl.BlockSpec((1, d_model), lambda b, i: (0, 0)),
        ],
        out_specs=pl.BlockSpec((None, tm, d_model), lambda b, i: (b, i, 0)),
        out_shape=jax.ShapeDtypeStruct(x.shape, x.dtype),
        compiler_params=pltpu.CompilerParams(
            dimension_semantics=("arbitrary", "arbitrary"), vmem_limit_bytes=VMEM_LIMIT),
        name="out_proj",
    )(conv, ga, yb, x, w_out, post_norm_w)


def kernel(x, pre_norm_w, w_in, conv_w, conv_b, filt_w1, filt_b1, filt_freq1, filt_w2, filt_b2, filt_freq2, filt_w3, filt_b3, filt_freq3, filt_w_out, hyena_skip, sgu_norm_w, sgu_norm_b, sgu_w, sgu_b, w_out, post_norm_w):
    batch, seq_len, d_model = x.shape
    d_a = hyena_skip.shape[0]
    n_heads, chunk, _ = sgu_w.shape
    d_b = sgu_norm_w.shape[0]
    head_dim = d_b // n_heads
    n_ct = d_a // CH_TILE
    assert 2 * seq_len == N_FFT and d_a % CH_TILE == 0 and head_dim == LANE
    assert seq_len % TOKEN_TILE == 0 and TOKEN_TILE % chunk == 0 and n_heads % 2 == 0
    assert seq_len % OUT_TOKEN_TILE == 0
    assert w_in.shape[1] == 4 * d_a + 3 * d_b

    g_filt, g_pair, g_inv_pair, f_fwd, f_inv = (
        jnp.asarray(t).astype(BF16) for t in _dft_tables())

    row = lambda v: v.astype(F32).reshape(1, -1)
    twice = lambda w: jnp.kron(jnp.eye(2, dtype=F32), w.astype(F32))
    tiled = lambda v: jnp.tile(row(v), (1, 2))
    assert filt_w2.shape[0] == EMB_PAD
    w1 = jnp.pad(filt_w1.astype(F32), ((0, EMB_PAD - FILTER_EMB), (0, 0)))
    kf, kb, l1 = _filter_mlp(twice(w1), tiled(filt_b1), tiled(filt_freq1),
                             twice(filt_w2), tiled(filt_b2), tiled(filt_freq2),
                             twice(filt_w3), tiled(filt_b3), tiled(filt_freq3),
                             twice(filt_w_out).astype(BF16), seq_len, d_a)
    khat = _filter_fft(kf, kb, g_filt, f_fwd, l1, row(hyena_skip))

    cw = jnp.transpose(conv_w.astype(F32).reshape(3, 3, d_a), (1, 0, 2)).reshape(9, d_a)
    cb = conv_b.astype(F32).reshape(3, d_a)
    sb = jnp.broadcast_to(sgu_b.astype(F32)[:, :, None], (n_heads, chunk, 1))

    u, ga, yb = _in_proj(x, row(pre_norm_w), w_in.astype(BF16), cw, cb, row(sgu_norm_w),
                         row(sgu_norm_b), sgu_w.astype(BF16), sb,
                         n_ct, n_heads, head_dim, chunk)
    conv = _long_conv(u, khat, g_pair, g_inv_pair, f_fwd, f_inv)
    return _out_proj(conv, ga, yb, x, w_out.astype(BF16), row(post_norm_w))
```

```python
import functools
import math

import numpy as np
import jax
import jax.numpy as jnp
from jax import lax
from jax.experimental import pallas as pl
from jax.experimental.pallas import tpu as pltpu

F32 = jnp.float32
BF16 = jnp.bfloat16

EPS = 1e-6
DECAY_TARGET = 1e-2
FAST_DECAY_PCT = 0.3
SLOW_DECAY_PCT = 1.5
FILTER_EMB = 33
EMB_PAD = 64

NB = 128
N_FFT = NB * NB
K1_COUNT = NB // 2 + 1
STAGE_ROWS = 144
SLAB_ROWS = STAGE_ROWS // 2
PAD_ROWS = 8
PAIR_PITCH = NB // 2 + PAD_ROWS
CH_TILE = 256
LANE = 128
TOKEN_TILE = 1024
OUT_TOKEN_TILE = 1024
HALO = 8
FILTER_GROUP = 16
STAGE_UNROLL = 16
SPECTRUM_UNROLL = 13
VMEM_LIMIT = 60 * 1024 * 1024


@functools.lru_cache(maxsize=None)
def _dft_tables():
    k1 = np.arange(K1_COUNT, dtype=np.int64)
    n1 = np.arange(NB, dtype=np.int64)
    n2 = np.arange(NB, dtype=np.int64)
    phase = (k1[None, :, None] * (NB * n1[None, None, :] + n2[:, None, None])) % N_FFT
    theta = 2.0 * np.pi * phase.astype(np.float64) / N_FFT
    g = np.zeros((NB, STAGE_ROWS, NB), np.float64)
    g[:, 0:2 * K1_COUNT:2, :] = np.cos(theta)
    g[:, 1:2 * K1_COUNT:2, :] = -np.sin(theta)
    herm = np.full((K1_COUNT,), 2.0)
    herm[0] = 1.0
    herm[-1] = 1.0
    row_w = np.zeros((STAGE_ROWS,), np.float64)
    row_w[:2 * K1_COUNT] = np.repeat(herm / N_FFT, 2)
    g_inv = np.transpose(g[:, :, :NB // 2] * row_w[None, :, None], (0, 2, 1))
    a = np.arange(NB, dtype=np.int64)
    ang = 2.0 * np.pi * ((a[:, None] * a[None, :]) % NB).astype(np.float64) / NB
    c, s = np.cos(ang), np.sin(ang)
    f_fwd = np.block([[c, s], [-s, c]]).reshape(2 * NB, 2, NB)
    f_fwd = np.transpose(f_fwd, (0, 2, 1)).reshape(2 * NB, 2 * NB)
    f_inv = np.block([[c, -s], [s, c]]).reshape(2, NB, 2 * NB)
    f_inv = np.transpose(f_inv, (1, 0, 2)).reshape(2 * NB, 2 * NB)
    half = NB // 2
    g_pair = np.zeros((half, 2, STAGE_ROWS, half, 2), np.float64)
    g_inv_pair = np.zeros((half, half, 2, 2, STAGE_ROWS), np.float64)
    for p in range(2):
        g_pair[:, p, :, :, p] = g[p::2, :, :half]
        g_inv_pair[:, :, p, p, :] = g_inv[p::2]
    g_pair = g_pair.reshape(half, 2 * STAGE_ROWS, NB)
    g_inv_pair = g_inv_pair.reshape(half, NB, 2 * STAGE_ROWS)
    g_filt = g.copy()
    g_filt[1:, :, half:] = g[1:, :, :half - 1:-1]
    g_filt[0, :, half] = 0.0
    g_filt[0, :, half + 1:] = g[0, :, :half:-1]
    return tuple(t.astype(np.float32) for t in (g_filt, g_pair, g_inv_pair, f_fwd, f_inv))


def _dot(a, b):
    return jnp.dot(a, b, preferred_element_type=F32)


def _dot_halves(a, b):
    cut = (a.shape[0] // 32) * 16
    return jnp.concatenate([_dot(a[:cut], b), _dot(a[cut:], b)], axis=0)


def _dot_split(a, b):
    a_hi = a.astype(BF16)
    b_hi = b.astype(BF16)
    a_lo = (a - a_hi.astype(F32)).astype(BF16)
    b_lo = (b - b_hi.astype(F32)).astype(BF16)
    return _dot(a_hi, b_hi) + _dot(a_lo, b_hi) + _dot(a_hi, b_lo)


def _silu(x):
    return x / (1.0 + jnp.exp(-x))


def _rms_norm(x, w):
    return x * lax.rsqrt(jnp.mean(x * x, axis=-1, keepdims=True) + EPS) * w


def _filter_mlp_kernel(z_ref, ta_ref, tb_ref, delta_ref, w1_ref, b1_ref, f1_ref, w2_ref, b2_ref,
                       f2_ref, w3_ref, b3_ref, f3_ref, wo_ref, kf_ref, kb_ref, l1_ref, *, d_a):
    step = pl.program_id(0)
    n_ct = kf_ref.shape[0]
    quarter = NB // 4
    h = jnp.sin(f1_ref[...] * (_dot_split(z_ref[...], w1_ref[...]) + b1_ref[...]))
    h = jnp.sin(f2_ref[...] * (_dot_split(h, w2_ref[...]) + b2_ref[...]))
    h = jnp.sin(f3_ref[...] * (_dot_split(h, w3_ref[...]) + b3_ref[...])).astype(BF16)

    @pl.when(step == 0)
    def _():
        l1_ref[...] = jnp.zeros_like(l1_ref)

    row = lax.broadcasted_iota(jnp.int32, (h.shape[0], CH_TILE), 0)
    no_tap = jnp.logical_and(row == 0, step == 0)
    for half, t_ref in enumerate((ta_ref, tb_ref)):
        for ct in range(n_ct):
            cols = slice(ct * CH_TILE, (ct + 1) * CH_TILE)
            decay = jnp.exp(-t_ref[...] * delta_ref[:, cols])
            for direction, out_ref in enumerate((kf_ref, kb_ref)):
                col0 = (half * 2 + direction) * d_a + ct * CH_TILE
                val = _dot(h, wo_ref[:, col0:col0 + CH_TILE]) * decay
                if direction == 1 and half == 0:
                    val = jnp.where(no_tap, 0.0, val)
                l1_ref[:, cols] += jnp.sum(jnp.abs(val), axis=0, keepdims=True)
                for g in range(FILTER_GROUP):
                    out_ref[ct, half * quarter:(half + 1) * quarter,
                            g * CH_TILE:(g + 1) * CH_TILE] = (
                                val[g * quarter:(g + 1) * quarter].astype(BF16))


@functools.lru_cache(maxsize=None)
def _filter_position_tables(seq_len, d_a):
    steps = NB // FILTER_GROUP
    quarter = NB // 4
    step, g, q, half = np.meshgrid(np.arange(steps), np.arange(FILTER_GROUP), np.arange(quarter),
                                   np.arange(2), indexing="ij")
    pos = (NB * (half * quarter + q) + step * FILTER_GROUP + g).astype(np.float64)
    bands = (FILTER_EMB - 1) // 2
    t = pos / (seq_len - 1)
    w = 2.0 * math.pi * pos / seq_len
    f = np.linspace(1e-4, bands - 1, bands)
    z = np.zeros(pos.shape + (EMB_PAD,), np.float64)
    z[..., 0] = t
    z[..., 1:1 + bands] = np.cos(f * w[..., None])
    z[..., 1 + bands:1 + 2 * bands] = -np.sin(f * w[..., None])
    rows = steps * FILTER_GROUP * quarter
    deltas = np.abs(np.linspace(math.log(DECAY_TARGET) / SLOW_DECAY_PCT,
                                math.log(DECAY_TARGET) / FAST_DECAY_PCT, d_a))
    t = t.reshape(rows, 2)
    return (z.reshape(rows, 2 * EMB_PAD).astype(np.float32), t[:, 0:1].astype(np.float32),
            t[:, 1:2].astype(np.float32), deltas[None, :].astype(np.float32))


def _filter_mlp(w1, b1, f1, w2, b2, f2, w3, b3, f3, wo, seq_len, d_a):
    n_ct = d_a // CH_TILE
    steps = NB // FILTER_GROUP
    rows = FILTER_GROUP * NB // 4
    width = w2.shape[0]
    z2, ta, tb, deltas = _filter_position_tables(seq_len, d_a)
    const = lambda shape: pl.BlockSpec(shape, lambda i: (0,) * len(shape))
    taps = pl.BlockSpec((n_ct, NB // 2, FILTER_GROUP * CH_TILE), lambda i: (0, 0, i))
    return pl.pallas_call(
        functools.partial(_filter_mlp_kernel, d_a=d_a),
        grid=(steps,),
        in_specs=[
            pl.BlockSpec((rows, width), lambda i: (i, 0)),
            pl.BlockSpec((rows, 1), lambda i: (i, 0)),
            pl.BlockSpec((rows, 1), lambda i: (i, 0)),
            const((1, d_a)),
            const((width, width)), const((1, width)), const((1, width)),
            const((width, width)), const((1, width)), const((1, width)),
            const((width, width)), const((1, width)), const((1, width)),
            const(wo.shape),
        ],
        out_specs=[taps, taps, pl.BlockSpec((1, d_a), lambda i: (0, 0))],
        out_shape=[
            jax.ShapeDtypeStruct((n_ct, NB // 2, NB * CH_TILE), BF16),
            jax.ShapeDtypeStruct((n_ct, NB // 2, NB * CH_TILE), BF16),
            jax.ShapeDtypeStruct((1, d_a), F32),
        ],
        compiler_params=pltpu.CompilerParams(dimension_semantics=("arbitrary",)),
        name="filter_mlp",
    )(z2, ta, tb, deltas, w1, b1, f1, w2, b2, f2, w3, b3, f3, wo)


def _pack_rows(x):
    return pltpu.bitcast(x.astype(BF16), jnp.uint32)


def _store_slab(s_ref, n2, res):
    row = pl.multiple_of(n2 * SLAB_ROWS, 8)
    w = _pack_rows(res)
    s_ref[0, pl.ds(row, SLAB_ROWS), :] = w[:, :LANE]
    s_ref[1, pl.ds(row, SLAB_ROWS), :] = w[:, LANE:]


def _load_slab(s_ref, n2):
    row = pl.multiple_of(n2 * SLAB_ROWS, 8)
    w = jnp.concatenate(
        [s_ref[0, pl.ds(row, SLAB_ROWS), :], s_ref[1, pl.ds(row, SLAB_ROWS), :]], axis=1)
    return pltpu.bitcast(w, BF16)


def _stage_one_filter(kf_ref, kb_ref, g_ref, s_ref):
    def body(n2, carry):
        col = pl.multiple_of(n2 * CH_TILE, CH_TILE)
        mirror = pl.multiple_of(((NB - n2) & (NB - 1)) * CH_TILE, CH_TILE)
        src = jnp.concatenate(
            [kf_ref[:, pl.ds(col, CH_TILE)], kb_ref[:, pl.ds(mirror, CH_TILE)]], axis=0)
        _store_slab(s_ref, n2, _dot(g_ref[n2], src))
        return carry

    lax.fori_loop(0, NB, body, 0, unroll=STAGE_UNROLL)


def _store_time_rows(ref, words):
    half = NB // 2
    for j in range(words.shape[0] // half):
        ref[j * PAIR_PITCH:j * PAIR_PITCH + half, :] = words[j * half:(j + 1) * half]
        ref[j * PAIR_PITCH + half:(j + 1) * PAIR_PITCH, :] = (
            words[(j + 1) * half - PAD_ROWS:(j + 1) * half])


def _load_time_rows(ref):
    half = NB // 2
    return jnp.concatenate(
        [ref[j * PAIR_PITCH:j * PAIR_PITCH + half, :] for j in range(ref.shape[0] // PAIR_PITCH)],
        axis=0)


def _stage_one_packed(u_ref, g_ref, s_ref):
    def body(m, carry):
        w = jnp.concatenate(
            [u_ref.at[half][pl.ds(m, NB // 2, stride=PAIR_PITCH), :] for half in range(2)], axis=1)
        res = _dot(g_ref[m], pltpu.bitcast(w, BF16))
        _store_slab(s_ref, 2 * m, res[:STAGE_ROWS])
        _store_slab(s_ref, 2 * m + 1, res[STAGE_ROWS:])
        return carry

    lax.fori_loop(0, NB // 2, body, 0, unroll=STAGE_UNROLL // 2)


def _load_k1(s_ref, k1):
    w = jnp.concatenate(
        [s_ref.at[half][pl.ds(k1, NB, stride=SLAB_ROWS), :] for half in range(2)], axis=1)
    return pltpu.bitcast(w, BF16)


def _store_k1(s_ref, k1, val):
    w = _pack_rows(val)
    for half in range(2):
        s_ref.at[half][pl.ds(k1, NB, stride=SLAB_ROWS), :] = w[:, half * LANE:(half + 1) * LANE]


def _filter_fft_kernel(kf_ref, kb_ref, g_ref, f_ref, l1_ref, skip_ref, khat_ref, s_ref):
    _stage_one_filter(kf_ref, kb_ref, g_ref, s_ref)
    scale = 1.0 / (l1_ref[...] + EPS)
    is_real = lax.broadcasted_iota(jnp.int32, (2 * NB, CH_TILE), 0) < NB
    shift = jnp.where(is_real, skip_ref[...], 0.0)

    def body(k1, carry):
        x = _dot(f_ref[...], _load_k1(s_ref, k1))
        khat_ref[k1] = (x * scale + shift).astype(BF16)
        return carry

    lax.fori_loop(0, K1_COUNT, body, 0, unroll=SPECTRUM_UNROLL)


def _filter_fft(kf, kb, g_filt, f_fwd, l1, skip):
    n_ct = kf.shape[0]
    taps = pl.BlockSpec((None, NB // 2, NB * CH_TILE), lambda c: (c, 0, 0))
    return pl.pallas_call(
        _filter_fft_kernel,
        grid=(n_ct,),
        in_specs=[
            taps, taps,
            pl.BlockSpec((NB, STAGE_ROWS, NB), lambda c: (0, 0, 0),
                         pipeline_mode=pl.Buffered(1)),
            pl.BlockSpec((2 * NB, 2 * NB), lambda c: (0, 0)),
            pl.BlockSpec((1, CH_TILE), lambda c: (0, c)),
            pl.BlockSpec((1, CH_TILE), lambda c: (0, c)),
        ],
        out_specs=pl.BlockSpec((None, K1_COUNT, 2 * NB, CH_TILE), lambda c: (c, 0, 0, 0)),
        out_shape=jax.ShapeDtypeStruct((n_ct, K1_COUNT, 2 * NB, CH_TILE), BF16),
        scratch_shapes=[pltpu.VMEM((2, NB * SLAB_ROWS, LANE), jnp.uint32)],
        compiler_params=pltpu.CompilerParams(
            dimension_semantics=("arbitrary",), vmem_limit_bytes=VMEM_LIMIT),
        name="filter_fft",
    )(kf, kb, g_filt, f_fwd, l1, skip)


def _long_conv_kernel(u_ref, khat_ref, g_ref, ginv_ref, f_ref, finv_ref, o_ref, s_ref):
    _stage_one_packed(u_ref, g_ref, s_ref)

    def spectrum_body(k1, carry):
        x = _dot(f_ref[...], _load_k1(s_ref, k1)).astype(BF16)
        kh = khat_ref[k1]
        xr, xi = x[:NB], x[NB:]
        kr, ki = kh[:NB], kh[NB:]
        y = jnp.concatenate([xr * kr - xi * ki, xr * ki + xi * kr], axis=0)
        _store_k1(s_ref, k1, _dot(finv_ref[...], y))
        return carry

    lax.fori_loop(0, K1_COUNT, spectrum_body, 0, unroll=SPECTRUM_UNROLL)

    def inverse_body(m, carry):
        b = jnp.concatenate([_load_slab(s_ref, 2 * m), _load_slab(s_ref, 2 * m + 1)], axis=0)
        w = _pack_rows(_dot(ginv_ref[m], b))
        for half in range(2):
            o_ref.at[half][pl.ds(m, NB // 2, stride=PAIR_PITCH), :] = (
                w[:, half * LANE:(half + 1) * LANE])
        return carry

    lax.fori_loop(0, NB // 2, inverse_body, 0, unroll=STAGE_UNROLL // 2)

    for half in range(2):
        for r in range(NB // 2, PAIR_PITCH):
            o_ref.at[half][pl.ds(r, NB // 2, stride=PAIR_PITCH), :] = (
                o_ref.at[half][pl.ds(r - PAD_ROWS, NB // 2, stride=PAIR_PITCH), :])


def _long_conv(u, khat, g_pair, g_inv_pair, f_fwd, f_inv):
    batch, n_slab, half_len, _ = u.shape
    slabs = CH_TILE // LANE
    n_ct = n_slab // slabs
    const = lambda shape: pl.BlockSpec(shape, lambda c, b: (0,) * len(shape),
                                       pipeline_mode=pl.Buffered(1))
    return pl.pallas_call(
        _long_conv_kernel,
        grid=(n_ct, batch),
        in_specs=[
            pl.BlockSpec((None, slabs, half_len, LANE), lambda c, b: (b, c, 0, 0)),
            pl.BlockSpec((None, K1_COUNT, 2 * NB, CH_TILE), lambda c, b: (c, 0, 0, 0),
                         pipeline_mode=pl.Buffered(1)),
            const(g_pair.shape),
            const(g_inv_pair.shape),
            const((2 * NB, 2 * NB)),
            const((2 * NB, 2 * NB)),
        ],
        out_specs=pl.BlockSpec((None, slabs, half_len, LANE), lambda c, b: (b, c, 0, 0)),
        out_shape=jax.ShapeDtypeStruct(u.shape, jnp.uint32),
        scratch_shapes=[pltpu.VMEM((2, NB * SLAB_ROWS, LANE), jnp.uint32)],
        compiler_params=pltpu.CompilerParams(
            dimension_semantics=("arbitrary", "arbitrary"), vmem_limit_bytes=VMEM_LIMIT),
        name="long_conv",
    )(u, khat, g_pair, g_inv_pair, f_fwd, f_inv)


def _in_proj_kernel(x_ref, xp_ref, xn_ref, pnw_ref, w_ref, cw_ref, cb_ref,
                    lnw_ref, lnb_ref, sw_ref, sb_ref, u_ref, ga_ref, yb_ref,
                    *, n_ct, n_heads, head_dim, chunk):
    i = pl.program_id(1)
    last = pl.num_programs(1) - 1
    tm = x_ref.shape[0]
    pnw = pnw_ref[...]
    h_main = _rms_norm(x_ref[...], pnw)
    h_prev = jnp.where(i == 0, 0.0, _rms_norm(xp_ref[...], pnw))
    h_next = jnp.where(i == last, 0.0, _rms_norm(xn_ref[...], pnw))
    h_ext = jnp.concatenate([h_prev, h_main, h_next], axis=0).astype(BF16)
    h_bf = h_main.astype(BF16)
    rows = tm + 2 * HALO

    d_a = n_ct * CH_TILE
    d_b = n_heads * head_dim
    pair = 2 * head_dim
    n_chunks = tm // chunk
    slabs = CH_TILE // LANE
    big_dot = _dot

    def hyena_tile(ct):
        cols = slice(ct * CH_TILE, (ct + 1) * CH_TILE)
        conv = []
        for which in range(3):
            p = big_dot(h_ext, w_ref[:, which * d_a + ct * CH_TILE:which * d_a + (ct + 1) * CH_TILE])
            cw = cw_ref[which * 3:(which + 1) * 3, cols]
            conv.append(pltpu.roll(p, 1, axis=0)[HALO:HALO + tm] * cw[0:1]
                        + p[HALO:HALO + tm] * cw[1:2]
                        + pltpu.roll(p, rows - 1, axis=0)[HALO:HALO + tm] * cw[2:3]
                        + cb_ref[which:which + 1, cols])
        gate = big_dot(h_bf, w_ref[:, 3 * d_a + ct * CH_TILE:3 * d_a + (ct + 1) * CH_TILE])
        u = conv[2] * conv[1]
        for half in range(slabs):
            _store_time_rows(u_ref.at[ct * slabs + half],
                             _pack_rows(u[:, half * LANE:(half + 1) * LANE]))
        ga_ref[:, cols] = (conv[0] * _silu(gate)).astype(BF16)

    def gmlp_pair(hp):
        q = [big_dot(h_bf, w_ref[:, 4 * d_a + which * d_b + hp * pair:
                              4 * d_a + which * d_b + (hp + 1) * pair]) for which in range(3)]
        for sub in range(2):
            hd = hp * 2 + sub
            lanes = slice(hd * head_dim, (hd + 1) * head_dim)
            lo = sub * head_dim
            su = q[0][:, lo:lo + head_dim]
            sv = q[1][:, lo:lo + head_dim]
            sg = q[2][:, lo:lo + head_dim]
            mu = jnp.mean(sv, axis=-1, keepdims=True)
            d = sv - mu
            var = jnp.mean(d * d, axis=-1, keepdims=True)
            vn = (d * lax.rsqrt(var + EPS) * lnw_ref[:, lanes] + lnb_ref[:, lanes]).astype(BF16)
            wide = jnp.concatenate(
                [vn[c * chunk:(c + 1) * chunk] for c in range(n_chunks)], axis=1)
            mixed = _dot(sw_ref[hd], wide) + sb_ref[hd]
            mixed = jnp.concatenate(
                [mixed[:, c * head_dim:(c + 1) * head_dim] for c in range(n_chunks)], axis=0)
            yb_ref[:, lanes] = (su * mixed * _silu(sg)).astype(BF16)

    for k in range(max(n_ct, n_heads // 2)):
        if k < n_heads // 2:
            gmlp_pair(k)
        if k < n_ct:
            hyena_tile(k)


def _in_proj(x, pre_norm_w, w, cw, cb, lnw, lnb, sw, sb, n_ct, n_heads, head_dim, chunk):
    batch, seq_len, d_model = x.shape
    tm = TOKEN_TILE
    steps = seq_len // tm
    halo_blocks = seq_len // HALO
    per_tile = tm // HALO
    const = lambda shape: pl.BlockSpec(shape, lambda b, i: (0,) * len(shape))
    const1 = lambda shape: pl.BlockSpec(shape, lambda b, i: (0,) * len(shape),
                                        pipeline_mode=pl.Buffered(1))
    d_b = n_heads * head_dim
    d_a = n_ct * CH_TILE
    return pl.pallas_call(
        functools.partial(_in_proj_kernel, n_ct=n_ct, n_heads=n_heads, head_dim=head_dim,
                          chunk=chunk),
        grid=(batch, steps),
        in_specs=[
            pl.BlockSpec((None, tm, d_model), lambda b, i: (b, i, 0)),
            pl.BlockSpec((None, HALO, d_model),
                         lambda b, i: (b, jnp.maximum(i * per_tile - 1, 0), 0)),
            pl.BlockSpec((None, HALO, d_model),
                         lambda b, i: (b, jnp.minimum((i + 1) * per_tile, halo_blocks - 1), 0)),
            const((1, d_model)),
            const1(w.shape),
            const(cw.shape), const(cb.shape), const(lnw.shape), const(lnb.shape),
            const(sw.shape), const(sb.shape),
        ],
        out_specs=[
            pl.BlockSpec((None, d_a // LANE, tm // NB * PAIR_PITCH, LANE),
                         lambda b, i: (b, 0, i, 0)),
            pl.BlockSpec((None, tm, d_a), lambda b, i: (b, i, 0)),
            pl.BlockSpec((None, tm, d_b), lambda b, i: (b, i, 0)),
        ],
        out_shape=[
            jax.ShapeDtypeStruct((batch, d_a // LANE, seq_len // NB * PAIR_PITCH, LANE),
                                 jnp.uint32),
            jax.ShapeDtypeStruct((batch, seq_len, d_a), BF16),
            jax.ShapeDtypeStruct((batch, seq_len, d_b), BF16),
        ],
        compiler_params=pltpu.CompilerParams(
            dimension_semantics=("arbitrary", "arbitrary"), vmem_limit_bytes=VMEM_LIMIT),
        name="in_proj",
    )(x, x, x, pre_norm_w, w, cw, cb, lnw, lnb, sw, sb)


def _out_proj_kernel(c_ref, ga_ref, yb_ref, x_ref, w_ref, pw_ref, o_ref):
    conv = jnp.concatenate(
        [pltpu.bitcast(_load_time_rows(c_ref.at[s]), BF16) for s in range(c_ref.shape[0])], axis=1)
    ya = (conv.astype(F32) * ga_ref[...].astype(F32)).astype(BF16)
    yc = jnp.concatenate([ya, yb_ref[...]], axis=1)
    y = _dot(yc, w_ref[...])
    o_ref[...] = x_ref[...] + _rms_norm(y, pw_ref[...])


def _out_proj(conv, ga, yb, x, w_out, post_norm_w):
    batch, seq_len, d_model = x.shape
    n_slab = conv.shape[1]
    tm = OUT_TOKEN_TILE
    return pl.pallas_call(
        _out_proj_kernel,
        grid=(batch, seq_len // tm),
        in_specs=[
            pl.BlockSpec((None, n_slab, tm // NB * PAIR_PITCH, LANE), lambda b, i: (b, 0, i, 0)),
            pl.BlockSpec((None, tm, ga.shape[-1]), lambda b, i: (b, i, 0)),
            pl.BlockSpec((None, tm, yb.shape[-1]), lambda b, i: (b, i, 0)),
            pl.BlockSpec((None, tm, d_model), lambda b, i: (b, i, 0)),
            pl.BlockSpec(w_out.shape, lambda b, i: (0, 0)),
            pl.BlockSpec((1, d_model), lambda b, i: (0, 0)),
        ],
        out_specs=pl.BlockSpec((None, tm, d_model), lambda b, i: (b, i, 0)),
        out_shape=jax.ShapeDtypeStruct(x.shape, x.dtype),
        compiler_params=pltpu.CompilerParams(
            dimension_semantics=("arbitrary", "arbitrary"), vmem_limit_bytes=VMEM_LIMIT),
        name="out_proj",
    )(conv, ga, yb, x, w_out, post_norm_w)


def kernel(x, pre_norm_w, w_in, conv_w, conv_b, filt_w1, filt_b1, filt_freq1, filt_w2, filt_b2, filt_freq2, filt_w3, filt_b3, filt_freq3, filt_w_out, hyena_skip, sgu_norm_w, sgu_norm_b, sgu_w, sgu_b, w_out, post_norm_w):
    batch, seq_len, d_model = x.shape
    d_a = hyena_skip.shape[0]
    n_heads, chunk, _ = sgu_w.shape
    d_b = sgu_norm_w.shape[0]
    head_dim = d_b // n_heads
    n_ct = d_a // CH_TILE
    assert 2 * seq_len == N_FFT and d_a % CH_TILE == 0 and head_dim == LANE
    assert seq_len % TOKEN_TILE == 0 and TOKEN_TILE % chunk == 0 and n_heads % 2 == 0
    assert seq_len % OUT_TOKEN_TILE == 0
    assert w_in.shape[1] == 4 * d_a + 3 * d_b

    g_filt, g_pair, g_inv_pair, f_fwd, f_inv = (
        jnp.asarray(t).astype(BF16) for t in _dft_tables())

    row = lambda v: v.astype(F32).reshape(1, -1)
    twice = lambda w: jnp.kron(jnp.eye(2, dtype=F32), w.astype(F32))
    tiled = lambda v: jnp.tile(row(v), (1, 2))
    assert filt_w2.shape[0] == EMB_PAD
    w1 = jnp.pad(filt_w1.astype(F32), ((0, EMB_PAD - FILTER_EMB), (0, 0)))
    kf, kb, l1 = _filter_mlp(twice(w1), tiled(filt_b1), tiled(filt_freq1),
                             twice(filt_w2), tiled(filt_b2), tiled(filt_freq2),
                             twice(filt_w3), tiled(filt_b3), tiled(filt_freq3),
                             twice(filt_w_out).astype(BF16), seq_len, d_a)
    khat = _filter_fft(kf, kb, g_filt, f_fwd, l1, row(hyena_skip))

    cw = jnp.transpose(conv_w.astype(F32).reshape(3, 3, d_a), (1, 0, 2)).reshape(9, d_a)
    cb = conv_b.astype(F32).reshape(3, d_a)
    sb = jnp.broadcast_to(sgu_b.astype(F32)[:, :, None], (n_heads, chunk, 1))

    u, ga, yb = _in_proj(x, row(pre_norm_w), w_in.astype(BF16), cw, cb, row(sgu_norm_w),
                         row(sgu_norm_b), sgu_w.astype(BF16), sb,
                         n_ct, n_heads, head_dim, chunk)
    conv = _long_conv(u, khat, g_pair, g_inv_pair, f_fwd, f_inv)
    return _out_proj(conv, ga, yb, x, w_out.astype(BF16), row(post_norm_w))
```

```python
import functools
import math

import numpy as np
import jax
import jax.numpy as jnp
from jax import lax
from jax.experimental import pallas as pl
from jax.experimental.pallas import tpu as pltpu

F32 = jnp.float32
BF16 = jnp.bfloat16

EPS = 1e-6
DECAY_TARGET = 1e-2
FAST_DECAY_PCT = 0.3
SLOW_DECAY_PCT = 1.5
FILTER_EMB = 33
EMB_PAD = 64

NB = 128
N_FFT = NB * NB
K1_COUNT = NB // 2 + 1
STAGE_ROWS = 144
SLAB_ROWS = STAGE_ROWS // 2
PAD_ROWS = 8
PAIR_PITCH = NB // 2 + PAD_ROWS
CH_TILE = 256
LANE = 128
TOKEN_TILE = 1024
OUT_TOKEN_TILE = 1024
HALO = 8
FILTER_GROUP = 16
STAGE_UNROLL = 128
SPECTRUM_UNROLL = 65
MATMUL_LOOKAHEAD = 1
VMEM_LIMIT = 60 * 1024 * 1024


@functools.lru_cache(maxsize=None)
def _dft_tables():
    k1 = np.arange(K1_COUNT, dtype=np.int64)
    n1 = np.arange(NB, dtype=np.int64)
    n2 = np.arange(NB, dtype=np.int64)
    phase = (k1[None, :, None] * (NB * n1[None, None, :] + n2[:, None, None])) % N_FFT
    theta = 2.0 * np.pi * phase.astype(np.float64) / N_FFT
    g = np.zeros((NB, STAGE_ROWS, NB), np.float64)
    g[:, 0:2 * K1_COUNT:2, :] = np.cos(theta)
    g[:, 1:2 * K1_COUNT:2, :] = -np.sin(theta)
    herm = np.full((K1_COUNT,), 2.0)
    herm[0] = 1.0
    herm[-1] = 1.0
    row_w = np.zeros((STAGE_ROWS,), np.float64)
    row_w[:2 * K1_COUNT] = np.repeat(herm / N_FFT, 2)
    g_inv = np.transpose(g[:, :, :NB // 2] * row_w[None, :, None], (0, 2, 1))
    a = np.arange(NB, dtype=np.int64)
    ang = 2.0 * np.pi * ((a[:, None] * a[None, :]) % NB).astype(np.float64) / NB
    c, s = np.cos(ang), np.sin(ang)
    f_fwd = np.block([[c, s], [-s, c]]).reshape(2 * NB, 2, NB)
    f_fwd = np.transpose(f_fwd, (0, 2, 1)).reshape(2 * NB, 2 * NB)
    f_inv = np.block([[c, -s], [s, c]]).reshape(2, NB, 2 * NB)
    f_inv = np.transpose(f_inv, (1, 0, 2)).reshape(2 * NB, 2 * NB)
    half = NB // 2
    g_pair = np.zeros((half, 2, STAGE_ROWS, half, 2), np.float64)
    g_inv_pair = np.zeros((half, half, 2, 2, STAGE_ROWS), np.float64)
    for p in range(2):
        g_pair[:, p, :, :, p] = g[p::2, :, :half]
        g_inv_pair[:, :, p, p, :] = g_inv[p::2]
    g_pair = g_pair.reshape(half, 2 * STAGE_ROWS, NB)
    g_inv_pair = g_inv_pair.reshape(half, NB, 2 * STAGE_ROWS)
    g_filt = g.copy()
    g_filt[1:, :, half:] = g[1:, :, :half - 1:-1]
    g_filt[0, :, half] = 0.0
    g_filt[0, :, half + 1:] = g[0, :, :half:-1]
    return tuple(t.astype(np.float32) for t in (g_filt, g_pair, g_inv_pair, f_fwd, f_inv))


def _dot(a, b):
    return jnp.dot(a, b, preferred_element_type=F32)


def _dot_halves(a, b):
    cut = (a.shape[0] // 32) * 16
    return jnp.concatenate([_dot(a[:cut], b), _dot(a[cut:], b)], axis=0)


def _dot_split(a, b):
    a_hi = a.astype(BF16)
    b_hi = b.astype(BF16)
    a_lo = (a - a_hi.astype(F32)).astype(BF16)
    b_lo = (b - b_hi.astype(F32)).astype(BF16)
    return _dot(a_hi, b_hi) + _dot(a_lo, b_hi) + _dot(a_hi, b_lo)


def _silu(x):
    return x / (1.0 + jnp.exp(-x))


def _rms_norm(x, w):
    return x * lax.rsqrt(jnp.mean(x * x, axis=-1, keepdims=True) + EPS) * w


def _filter_mlp_kernel(z_ref, ta_ref, tb_ref, delta_ref, w1_ref, b1_ref, f1_ref, w2_ref, b2_ref,
                       f2_ref, w3_ref, b3_ref, f3_ref, wo_ref, kf_ref, kb_ref, l1_ref, *, d_a):
    step = pl.program_id(0)
    n_ct = kf_ref.shape[0]
    quarter = NB // 4
    h = jnp.sin(f1_ref[...] * (_dot_split(z_ref[...], w1_ref[...]) + b1_ref[...]))
    h = jnp.sin(f2_ref[...] * (_dot_split(h, w2_ref[...]) + b2_ref[...]))
    h = jnp.sin(f3_ref[...] * (_dot_split(h, w3_ref[...]) + b3_ref[...])).astype(BF16)

    @pl.when(step == 0)
    def _():
        l1_ref[...] = jnp.zeros_like(l1_ref)

    row = lax.broadcasted_iota(jnp.int32, (h.shape[0], CH_TILE), 0)
    no_tap = jnp.logical_and(row == 0, step == 0)
    for half, t_ref in enumerate((ta_ref, tb_ref)):
        for ct in range(n_ct):
            cols = slice(ct * CH_TILE, (ct + 1) * CH_TILE)
            decay = jnp.exp(-t_ref[...] * delta_ref[:, cols])
            for direction, out_ref in enumerate((kf_ref, kb_ref)):
                col0 = (half * 2 + direction) * d_a + ct * CH_TILE
                val = _dot(h, wo_ref[:, col0:col0 + CH_TILE]) * decay
                if direction == 1 and half == 0:
                    val = jnp.where(no_tap, 0.0, val)
                l1_ref[:, cols] += jnp.sum(jnp.abs(val), axis=0, keepdims=True)
                for g in range(FILTER_GROUP):
                    out_ref[ct, half * quarter:(half + 1) * quarter,
                            g * CH_TILE:(g + 1) * CH_TILE] = (
                                val[g * quarter:(g + 1) * quarter].astype(BF16))


@functools.lru_cache(maxsize=None)
def _filter_position_tables(seq_len, d_a):
    steps = NB // FILTER_GROUP
    quarter = NB // 4
    step, g, q, half = np.meshgrid(np.arange(steps), np.arange(FILTER_GROUP), np.arange(quarter),
                                   np.arange(2), indexing="ij")
    pos = (NB * (half * quarter + q) + step * FILTER_GROUP + g).astype(np.float64)
    bands = (FILTER_EMB - 1) // 2
    t = pos / (seq_len - 1)
    w = 2.0 * math.pi * pos / seq_len
    f = np.linspace(1e-4, bands - 1, bands)
    z = np.zeros(pos.shape + (EMB_PAD,), np.float64)
    z[..., 0] = t
    z[..., 1:1 + bands] = np.cos(f * w[..., None])
    z[..., 1 + bands:1 + 2 * bands] = -np.sin(f * w[..., None])
    rows = steps * FILTER_GROUP * quarter
    deltas = np.abs(np.linspace(math.log(DECAY_TARGET) / SLOW_DECAY_PCT,
                                math.log(DECAY_TARGET) / FAST_DECAY_PCT, d_a))
    t = t.reshape(rows, 2)
    return (z.reshape(rows, 2 * EMB_PAD).astype(np.float32), t[:, 0:1].astype(np.float32),
            t[:, 1:2].astype(np.float32), deltas[None, :].astype(np.float32))


def _filter_mlp(w1, b1, f1, w2, b2, f2, w3, b3, f3, wo, seq_len, d_a):
    n_ct = d_a // CH_TILE
    steps = NB // FILTER_GROUP
    rows = FILTER_GROUP * NB // 4
    width = w2.shape[0]
    z2, ta, tb, deltas = _filter_position_tables(seq_len, d_a)
    const = lambda shape: pl.BlockSpec(shape, lambda i: (0,) * len(shape))
    taps = pl.BlockSpec((n_ct, NB // 2, FILTER_GROUP * CH_TILE), lambda i: (0, 0, i))
    return pl.pallas_call(
        functools.partial(_filter_mlp_kernel, d_a=d_a),
        grid=(steps,),
        in_specs=[
            pl.BlockSpec((rows, width), lambda i: (i, 0)),
            pl.BlockSpec((rows, 1), lambda i: (i, 0)),
            pl.BlockSpec((rows, 1), lambda i: (i, 0)),
            const((1, d_a)),
            const((width, width)), const((1, width)), const((1, width)),
            const((width, width)), const((1, width)), const((1, width)),
            const((width, width)), const((1, width)), const((1, width)),
            const(wo.shape),
        ],
        out_specs=[taps, taps, pl.BlockSpec((1, d_a), lambda i: (0, 0))],
        out_shape=[
            jax.ShapeDtypeStruct((n_ct, NB // 2, NB * CH_TILE), BF16),
            jax.ShapeDtypeStruct((n_ct, NB // 2, NB * CH_TILE), BF16),
            jax.ShapeDtypeStruct((1, d_a), F32),
        ],
        compiler_params=pltpu.CompilerParams(dimension_semantics=("arbitrary",)),
        name="filter_mlp",
    )(z2, ta, tb, deltas, w1, b1, f1, w2, b2, f2, w3, b3, f3, wo)


def _pack_rows(x):
    return pltpu.bitcast(x.astype(BF16), jnp.uint32)


def _store_slab(s_ref, n2, res):
    row = pl.multiple_of(n2 * SLAB_ROWS, 8)
    w = _pack_rows(res)
    s_ref[0, pl.ds(row, SLAB_ROWS), :] = w[:, :LANE]
    s_ref[1, pl.ds(row, SLAB_ROWS), :] = w[:, LANE:]


def _load_slab(s_ref, n2):
    row = pl.multiple_of(n2 * SLAB_ROWS, 8)
    w = jnp.concatenate(
        [s_ref[0, pl.ds(row, SLAB_ROWS), :], s_ref[1, pl.ds(row, SLAB_ROWS), :]], axis=1)
    return pltpu.bitcast(w, BF16)


def _stage_one_filter(kf_ref, kb_ref, g_ref, s_ref):
    def body(n2, carry):
        col = pl.multiple_of(n2 * CH_TILE, CH_TILE)
        mirror = pl.multiple_of(((NB - n2) & (NB - 1)) * CH_TILE, CH_TILE)
        src = jnp.concatenate(
            [kf_ref[:, pl.ds(col, CH_TILE)], kb_ref[:, pl.ds(mirror, CH_TILE)]], axis=0)
        _store_slab(s_ref, n2, _dot(g_ref[n2], src))
        return carry

    lax.fori_loop(0, NB, body, 0, unroll=STAGE_UNROLL)


def _store_time_rows(ref, words):
    half = NB // 2
    for j in range(words.shape[0] // half):
        ref[j * PAIR_PITCH:j * PAIR_PITCH + half, :] = words[j * half:(j + 1) * half]
        ref[j * PAIR_PITCH + half:(j + 1) * PAIR_PITCH, :] = (
            words[(j + 1) * half - PAD_ROWS:(j + 1) * half])


def _load_time_rows(ref):
    half = NB // 2
    return jnp.concatenate(
        [ref[j * PAIR_PITCH:j * PAIR_PITCH + half, :] for j in range(ref.shape[0] // PAIR_PITCH)],
        axis=0)


def _stage_one_packed(u_ref, g_ref, s_ref):
    def body(m, carry):
        w = jnp.concatenate(
            [u_ref.at[half][pl.ds(m, NB // 2, stride=PAIR_PITCH), :] for half in range(2)], axis=1)
        res = _dot(g_ref[m], pltpu.bitcast(w, BF16))
        _store_slab(s_ref, 2 * m, res[:STAGE_ROWS])
        _store_slab(s_ref, 2 * m + 1, res[STAGE_ROWS:])
        return carry

    lax.fori_loop(0, NB // 2, body, 0, unroll=STAGE_UNROLL // 2)


def _load_k1(s_ref, k1):
    w = jnp.concatenate(
        [s_ref.at[half][pl.ds(k1, NB, stride=SLAB_ROWS), :] for half in range(2)], axis=1)
    return pltpu.bitcast(w, BF16)


def _store_k1(s_ref, k1, val):
    w = _pack_rows(val)
    for half in range(2):
        s_ref.at[half][pl.ds(k1, NB, stride=SLAB_ROWS), :] = w[:, half * LANE:(half + 1) * LANE]


def _filter_fft_kernel(kf_ref, kb_ref, g_ref, f_ref, l1_ref, skip_ref, khat_ref, s_ref):
    _stage_one_filter(kf_ref, kb_ref, g_ref, s_ref)
    scale = 1.0 / (l1_ref[...] + EPS)
    is_real = lax.broadcasted_iota(jnp.int32, (2 * NB, CH_TILE), 0) < NB
    shift = jnp.where(is_real, skip_ref[...], 0.0)

    def body(k1, carry):
        x = _dot(f_ref[...], _load_k1(s_ref, k1))
        khat_ref[k1] = (x * scale + shift).astype(BF16)
        return carry

    lax.fori_loop(0, K1_COUNT, body, 0, unroll=SPECTRUM_UNROLL)


def _filter_fft(kf, kb, g_filt, f_fwd, l1, skip):
    n_ct = kf.shape[0]
    taps = pl.BlockSpec((None, NB // 2, NB * CH_TILE), lambda c: (c, 0, 0))
    return pl.pallas_call(
        _filter_fft_kernel,
        grid=(n_ct,),
        in_specs=[
            taps, taps,
            pl.BlockSpec((NB, STAGE_ROWS, NB), lambda c: (0, 0, 0),
                         pipeline_mode=pl.Buffered(1)),
            pl.BlockSpec((2 * NB, 2 * NB), lambda c: (0, 0)),
            pl.BlockSpec((1, CH_TILE), lambda c: (0, c)),
            pl.BlockSpec((1, CH_TILE), lambda c: (0, c)),
        ],
        out_specs=pl.BlockSpec((None, K1_COUNT, 2 * NB, CH_TILE), lambda c: (c, 0, 0, 0)),
        out_shape=jax.ShapeDtypeStruct((n_ct, K1_COUNT, 2 * NB, CH_TILE), BF16),
        scratch_shapes=[pltpu.VMEM((2, NB * SLAB_ROWS, LANE), jnp.uint32)],
        compiler_params=pltpu.CompilerParams(
            dimension_semantics=("arbitrary",), vmem_limit_bytes=VMEM_LIMIT),
        name="filter_fft",
    )(kf, kb, g_filt, f_fwd, l1, skip)


def _long_conv_kernel(u_ref, khat_ref, g_ref, ginv_ref, f_ref, finv_ref, o_ref, s_ref):
    _stage_one_packed(u_ref, g_ref, s_ref)

    def spectrum_body(k1, carry):
        x = _dot(f_ref[...], _load_k1(s_ref, k1)).astype(BF16)
        kh = khat_ref[k1]
        xr, xi = x[:NB], x[NB:]
        kr, ki = kh[:NB], kh[NB:]
        y = jnp.concatenate([xr * kr - xi * ki, xr * ki + xi * kr], axis=0)
        _store_k1(s_ref, k1, _dot(finv_ref[...], y))
        return carry

    lax.fori_loop(0, K1_COUNT, spectrum_body, 0, unroll=SPECTRUM_UNROLL)

    def inverse_body(m, carry):
        b = jnp.concatenate([_load_slab(s_ref, 2 * m), _load_slab(s_ref, 2 * m + 1)], axis=0)
        w = _pack_rows(_dot(ginv_ref[m], b))
        for half in range(2):
            o_ref.at[half][pl.ds(m, NB // 2, stride=PAIR_PITCH), :] = (
                w[:, half * LANE:(half + 1) * LANE])
        return carry

    lax.fori_loop(0, NB // 2, inverse_body, 0, unroll=STAGE_UNROLL // 2)

    for half in range(2):
        for r in range(NB // 2, PAIR_PITCH):
            o_ref.at[half][pl.ds(r, NB // 2, stride=PAIR_PITCH), :] = (
                o_ref.at[half][pl.ds(r - PAD_ROWS, NB // 2, stride=PAIR_PITCH), :])


def _long_conv(u, khat, g_pair, g_inv_pair, f_fwd, f_inv):
    batch, n_slab, half_len, _ = u.shape
    slabs = CH_TILE // LANE
    n_ct = n_slab // slabs
    const = lambda shape: pl.BlockSpec(shape, lambda c, b: (0,) * len(shape),
                                       pipeline_mode=pl.Buffered(1))
    return pl.pallas_call(
        _long_conv_kernel,
        grid=(n_ct, batch),
        in_specs=[
            pl.BlockSpec((None, slabs, half_len, LANE), lambda c, b: (b, c, 0, 0)),
            pl.BlockSpec((None, K1_COUNT, 2 * NB, CH_TILE), lambda c, b: (c, 0, 0, 0)),
            const(g_pair.shape),
            const(g_inv_pair.shape),
            const((2 * NB, 2 * NB)),
            const((2 * NB, 2 * NB)),
        ],
        out_specs=pl.BlockSpec((None, slabs, half_len, LANE), lambda c, b: (b, c, 0, 0)),
        out_shape=jax.ShapeDtypeStruct(u.shape, jnp.uint32),
        scratch_shapes=[pltpu.VMEM((2, NB * SLAB_ROWS, LANE), jnp.uint32)],
        compiler_params=pltpu.CompilerParams(
            dimension_semantics=("arbitrary", "arbitrary"), vmem_limit_bytes=VMEM_LIMIT),
        name="long_conv",
    )(u, khat, g_pair, g_inv_pair, f_fwd, f_inv)


def _in_proj_kernel(x_ref, xp_ref, xn_ref, pnw_ref, w_ref, cw_ref, cb_ref,
                    lnw_ref, lnb_ref, sw_ref, sb_ref, u_ref, ga_ref, yb_ref,
                    *, n_ct, n_heads, head_dim, chunk):
    i = pl.program_id(1)
    last = pl.num_programs(1) - 1
    tm = x_ref.shape[0]
    pnw = pnw_ref[...]
    h_main = _rms_norm(x_ref[...], pnw)
    h_prev = jnp.where(i == 0, 0.0, _rms_norm(xp_ref[...], pnw))
    h_next = jnp.where(i == last, 0.0, _rms_norm(xn_ref[...], pnw))
    h_ext = jnp.concatenate([h_prev, h_main, h_next], axis=0).astype(BF16)
    h_bf = h_main.astype(BF16)
    rows = tm + 2 * HALO

    d_a = n_ct * CH_TILE
    d_b = n_heads * head_dim
    pair = 2 * head_dim
    n_chunks = tm // chunk
    slabs = CH_TILE // LANE
    big_dot = _dot

    def hyena_dots(ct):
        proj = [big_dot(h_ext, w_ref[:, which * d_a + ct * CH_TILE:
                                     which * d_a + (ct + 1) * CH_TILE]) for which in range(3)]
        return proj + [big_dot(h_bf, w_ref[:, 3 * d_a + ct * CH_TILE:3 * d_a + (ct + 1) * CH_TILE])]

    def hyena_finish(ct, proj):
        cols = slice(ct * CH_TILE, (ct + 1) * CH_TILE)
        conv = []
        for which in range(3):
            p = proj[which]
            cw = cw_ref[which * 3:(which + 1) * 3, cols]
            conv.append(pltpu.roll(p, 1, axis=0)[HALO:HALO + tm] * cw[0:1]
                        + p[HALO:HALO + tm] * cw[1:2]
                        + pltpu.roll(p, rows - 1, axis=0)[HALO:HALO + tm] * cw[2:3]
                        + cb_ref[which:which + 1, cols])
        gate = proj[3]
        u = conv[2] * conv[1]
        for half in range(slabs):
            _store_time_rows(u_ref.at[ct * slabs + half],
                             _pack_rows(u[:, half * LANE:(half + 1) * LANE]))
        ga_ref[:, cols] = (conv[0] * _silu(gate)).astype(BF16)

    def gmlp_dots(hp):
        return [big_dot(h_bf, w_ref[:, 4 * d_a + which * d_b + hp * pair:
                                    4 * d_a + which * d_b + (hp + 1) * pair]) for which in range(3)]

    def gmlp_finish(hp, q):
        for sub in range(2):
            hd = hp * 2 + sub
            lanes = slice(hd * head_dim, (hd + 1) * head_dim)
            lo = sub * head_dim
            su = q[0][:, lo:lo + head_dim]
            sv = q[1][:, lo:lo + head_dim]
            sg = q[2][:, lo:lo + head_dim]
            mu = jnp.mean(sv, axis=-1, keepdims=True)
            d = sv - mu
            var = jnp.mean(d * d, axis=-1, keepdims=True)
            vn = (d * lax.rsqrt(var + EPS) * lnw_ref[:, lanes] + lnb_ref[:, lanes]).astype(BF16)
            wide = jnp.concatenate(
                [vn[c * chunk:(c + 1) * chunk] for c in range(n_chunks)], axis=1)
            mixed = _dot(sw_ref[hd], wide) + sb_ref[hd]
            mixed = jnp.concatenate(
                [mixed[:, c * head_dim:(c + 1) * head_dim] for c in range(n_chunks)], axis=0)
            yb_ref[:, lanes] = (su * mixed * _silu(sg)).astype(BF16)

    units = []
    for k in range(max(n_ct, n_heads // 2)):
        if k < n_heads // 2:
            units.append((gmlp_dots, gmlp_finish, k))
        if k < n_ct:
            units.append((hyena_dots, hyena_finish, k))
    issued = [units[j][0](units[j][2]) for j in range(min(MATMUL_LOOKAHEAD, len(units)))]
    for idx, (_, finish, k) in enumerate(units):
        nxt = idx + MATMUL_LOOKAHEAD
        if nxt < len(units):
            issued.append(units[nxt][0](units[nxt][2]))
        finish(k, issued[idx])


def _in_proj(x, pre_norm_w, w, cw, cb, lnw, lnb, sw, sb, n_ct, n_heads, head_dim, chunk):
    batch, seq_len, d_model = x.shape
    tm = TOKEN_TILE
    steps = seq_len // tm
    halo_blocks = seq_len // HALO
    per_tile = tm // HALO
    const = lambda shape: pl.BlockSpec(shape, lambda b, i: (0,) * len(shape))
    const1 = lambda shape: pl.BlockSpec(shape, lambda b, i: (0,) * len(shape),
                                        pipeline_mode=pl.Buffered(1))
    d_b = n_heads * head_dim
    d_a = n_ct * CH_TILE
    return pl.pallas_call(
        functools.partial(_in_proj_kernel, n_ct=n_ct, n_heads=n_heads, head_dim=head_dim,
                          chunk=chunk),
        grid=(batch, steps),
        in_specs=[
            pl.BlockSpec((None, tm, d_model), lambda b, i: (b, i, 0)),
            pl.BlockSpec((None, HALO, d_model),
                         lambda b, i: (b, jnp.maximum(i * per_tile - 1, 0), 0)),
            pl.BlockSpec((None, HALO, d_model),
                         lambda b, i: (b, jnp.minimum((i + 1) * per_tile, halo_blocks - 1), 0)),
            const((1, d_model)),
            const1(w.shape),
            const(cw.shape), const(cb.shape), const(lnw.shape), const(lnb.shape),
            const(sw.shape), const(sb.shape),
        ],
        out_specs=[
            pl.BlockSpec((None, d_a // LANE, tm // NB * PAIR_PITCH, LANE),
                         lambda b, i: (b, 0, i, 0)),
            pl.BlockSpec((None, tm, d_a), lambda b, i: (b, i, 0)),
            pl.BlockSpec((None, tm, d_b), lambda b, i: (b, i, 0)),
        ],
        out_shape=[
            jax.ShapeDtypeStruct((batch, d_a // LANE, seq_len // NB * PAIR_PITCH, LANE),
                                 jnp.uint32),
            jax.ShapeDtypeStruct((batch, seq_len, d_a), BF16),
            jax.ShapeDtypeStruct((batch, seq_len, d_b), BF16),
        ],
        compiler_params=pltpu.CompilerParams(
            dimension_semantics=("arbitrary", "arbitrary"), vmem_limit_bytes=VMEM_LIMIT),
        name="in_proj",
    )(x, x, x, pre_norm_w, w, cw, cb, lnw, lnb, sw, sb)


def _out_proj_kernel(c_ref, ga_ref, yb_ref, x_ref, w_ref, pw_ref, o_ref):
    conv = jnp.concatenate(
        [pltpu.bitcast(_load_time_rows(c_ref.at[s]), BF16) for s in range(c_ref.shape[0])], axis=1)
    ya = (conv.astype(F32) * ga_ref[...].astype(F32)).astype(BF16)
    yc = jnp.concatenate([ya, yb_ref[...]], axis=1)
    y = _dot(yc, w_ref[...])
    o_ref[...] = x_ref[...] + _rms_norm(y, pw_ref[...])


def _out_proj(conv, ga, yb, x, w_out, post_norm_w):
    batch, seq_len, d_model = x.shape
    n_slab = conv.shape[1]
    tm = OUT_TOKEN_TILE
    return pl.pallas_call(
        _out_proj_kernel,
        grid=(batch, seq_len // tm),
        in_specs=[
            pl.BlockSpec((None, n_slab, tm // NB * PAIR_PITCH, LANE), lambda b, i: (b, 0, i, 0)),
            pl.BlockSpec((None, tm, ga.shape[-1]), lambda b, i: (b, i, 0)),
            pl.BlockSpec((None, tm, yb.shape[-1]), lambda b, i: (b, i, 0)),
            pl.BlockSpec((None, tm, d_model), lambda b, i: (b, i, 0)),
            pl.BlockSpec(w_out.shape, lambda b, i: (0, 0)),
            pl.BlockSpec((1, d_model), lambda b, i: (0, 0)),
        ],
        out_specs=pl.BlockSpec((None, tm, d_model), lambda b, i: (b, i, 0)),
        out_shape=jax.ShapeDtypeStruct(x.shape, x.dtype),
        compiler_params=pltpu.CompilerParams(
            dimension_semantics=("arbitrary", "arbitrary"), vmem_limit_bytes=VMEM_LIMIT),
        name="out_proj",
    )(conv, ga, yb, x, w_out, post_norm_w)


def kernel(x, pre_norm_w, w_in, conv_w, conv_b, filt_w1, filt_b1, filt_freq1, filt_w2, filt_b2, filt_freq2, filt_w3, filt_b3, filt_freq3, filt_w_out, hyena_skip, sgu_norm_w, sgu_norm_b, sgu_w, sgu_b, w_out, post_norm_w):
    batch, seq_len, d_model = x.shape
    d_a = hyena_skip.shape[0]
    n_heads, chunk, _ = sgu_w.shape
    d_b = sgu_norm_w.shape[0]
    head_dim = d_b // n_heads
    n_ct = d_a // CH_TILE
    assert 2 * seq_len == N_FFT and d_a % CH_TILE == 0 and head_dim == LANE
    assert seq_len % TOKEN_TILE == 0 and TOKEN_TILE % chunk == 0 and n_heads % 2 == 0
    assert seq_len % OUT_TOKEN_TILE == 0
    assert w_in.shape[1] == 4 * d_a + 3 * d_b

    g_filt, g_pair, g_inv_pair, f_fwd, f_inv = (
        jnp.asarray(t).astype(BF16) for t in _dft_tables())

    row = lambda v: v.astype(F32).reshape(1, -1)
    twice = lambda w: jnp.kron(jnp.eye(2, dtype=F32), w.astype(F32))
    tiled = lambda v: jnp.tile(row(v), (1, 2))
    assert filt_w2.shape[0] == EMB_PAD
    w1 = jnp.pad(filt_w1.astype(F32), ((0, EMB_PAD - FILTER_EMB), (0, 0)))
    kf, kb, l1 = _filter_mlp(twice(w1), tiled(filt_b1), tiled(filt_freq1),
                             twice(filt_w2), tiled(filt_b2), tiled(filt_freq2),
                             twice(filt_w3), tiled(filt_b3), tiled(filt_freq3),
                             twice(filt_w_out).astype(BF16), seq_len, d_a)
    khat = _filter_fft(kf, kb, g_filt, f_fwd, l1, row(hyena_skip))

    cw = jnp.transpose(conv_w.astype(F32).reshape(3, 3, d_a), (1, 0, 2)).reshape(9, d_a)
    cb = conv_b.astype(F32).reshape(3, d_a)
    sb = jnp.broadcast_to(sgu_b.astype(F32)[:, :, None], (n_heads, chunk, 1))

    u, ga, yb = _in_proj(x, row(pre_norm_w), w_in.astype(BF16), cw, cb, row(sgu_norm_w),
                         row(sgu_norm_b), sgu_w.astype(BF16), sb,
                         n_ct, n_heads, head_dim, chunk)
    conv = _long_conv(u, khat, g_pair, g_inv_pair, f_fwd, f_inv)
    return _out_proj(conv, ga, yb, x, w_out.astype(BF16), row(post_norm_w))
```

```python
import functools
import math

import numpy as np
import jax
import jax.numpy as jnp
from jax import lax
from jax.experimental import pallas as pl
from jax.experimental.pallas import tpu as pltpu

F32 = jnp.float32
BF16 = jnp.bfloat16

EPS = 1e-6
DECAY_TARGET = 1e-2
FAST_DECAY_PCT = 0.3
SLOW_DECAY_PCT = 1.5
FILTER_EMB = 33
EMB_PAD = 64

NB = 128
N_FFT = NB * NB
K1_COUNT = NB // 2 + 1
STAGE_ROWS = 144
SLAB_ROWS = STAGE_ROWS // 2
PAD_ROWS = 8
PAIR_PITCH = NB // 2 + PAD_ROWS
CH_TILE = 256
LANE = 128
TOKEN_TILE = 1024
OUT_TOKEN_TILE = 1024
HALO = 8
FILTER_GROUP = 16
STAGE_UNROLL = 128
SPECTRUM_UNROLL = 65
MATMUL_LOOKAHEAD = 1
VMEM_LIMIT = 60 * 1024 * 1024


@functools.lru_cache(maxsize=None)
def _dft_tables():
    k1 = np.arange(K1_COUNT, dtype=np.int64)
    n1 = np.arange(NB, dtype=np.int64)
    n2 = np.arange(NB, dtype=np.int64)
    phase = (k1[None, :, None] * (NB * n1[None, None, :] + n2[:, None, None])) % N_FFT
    theta = 2.0 * np.pi * phase.astype(np.float64) / N_FFT
    g = np.zeros((NB, STAGE_ROWS, NB), np.float64)
    g[:, 0:2 * K1_COUNT:2, :] = np.cos(theta)
    g[:, 1:2 * K1_COUNT:2, :] = -np.sin(theta)
    herm = np.full((K1_COUNT,), 2.0)
    herm[0] = 1.0
    herm[-1] = 1.0
    row_w = np.zeros((STAGE_ROWS,), np.float64)
    row_w[:2 * K1_COUNT] = np.repeat(herm / N_FFT, 2)
    g_inv = np.transpose(g[:, :, :NB // 2] * row_w[None, :, None], (0, 2, 1))
    a = np.arange(NB, dtype=np.int64)
    ang = 2.0 * np.pi * ((a[:, None] * a[None, :]) % NB).astype(np.float64) / NB
    c, s = np.cos(ang), np.sin(ang)
    f_fwd = np.block([[c, s], [-s, c]]).reshape(2 * NB, 2, NB)
    f_fwd = np.transpose(f_fwd, (0, 2, 1)).reshape(2 * NB, 2 * NB)
    f_inv = np.block([[c, -s], [s, c]]).reshape(2, NB, 2 * NB)
    f_inv = np.transpose(f_inv, (1, 0, 2)).reshape(2 * NB, 2 * NB)
    half = NB // 2
    g_pair = np.zeros((half, 2, STAGE_ROWS, half, 2), np.float64)
    for p in range(2):
        g_pair[:, p, :, :, p] = g[p::2, :, :half]
    g_pair = g_pair.reshape(half, 2 * STAGE_ROWS, NB)
    g_filt = g.copy()
    g_filt[1:, :, half:] = g[1:, :, :half - 1:-1]
    g_filt[0, :, half] = 0.0
    g_filt[0, :, half + 1:] = g[0, :, :half:-1]
    return tuple(t.astype(np.float32) for t in (g_filt, g_pair, g_inv, f_fwd, f_inv))


def _dot(a, b):
    return jnp.dot(a, b, preferred_element_type=F32)


def _dot_halves(a, b):
    cut = (a.shape[0] // 32) * 16
    return jnp.concatenate([_dot(a[:cut], b), _dot(a[cut:], b)], axis=0)


def _dot_split(a, b):
    a_hi = a.astype(BF16)
    b_hi = b.astype(BF16)
    a_lo = (a - a_hi.astype(F32)).astype(BF16)
    b_lo = (b - b_hi.astype(F32)).astype(BF16)
    return _dot(a_hi, b_hi) + _dot(a_lo, b_hi) + _dot(a_hi, b_lo)


def _silu(x):
    return x / (1.0 + jnp.exp(-x))


def _rms_norm(x, w):
    return x * lax.rsqrt(jnp.mean(x * x, axis=-1, keepdims=True) + EPS) * w


def _filter_mlp_kernel(z_ref, ta_ref, tb_ref, delta_ref, w1_ref, b1_ref, f1_ref, w2_ref, b2_ref,
                       f2_ref, w3_ref, b3_ref, f3_ref, wo_ref, kf_ref, kb_ref, l1_ref, *, d_a):
    step = pl.program_id(0)
    n_ct = kf_ref.shape[0]
    quarter = NB // 4
    h = jnp.sin(f1_ref[...] * (_dot_split(z_ref[...], w1_ref[...]) + b1_ref[...]))
    h = jnp.sin(f2_ref[...] * (_dot_split(h, w2_ref[...]) + b2_ref[...]))
    h = jnp.sin(f3_ref[...] * (_dot_split(h, w3_ref[...]) + b3_ref[...])).astype(BF16)

    @pl.when(step == 0)
    def _():
        l1_ref[...] = jnp.zeros_like(l1_ref)

    row = lax.broadcasted_iota(jnp.int32, (h.shape[0], CH_TILE), 0)
    no_tap = jnp.logical_and(row == 0, step == 0)
    for half, t_ref in enumerate((ta_ref, tb_ref)):
        for ct in range(n_ct):
            cols = slice(ct * CH_TILE, (ct + 1) * CH_TILE)
            decay = jnp.exp(-t_ref[...] * delta_ref[:, cols])
            for direction, out_ref in enumerate((kf_ref, kb_ref)):
                col0 = (half * 2 + direction) * d_a + ct * CH_TILE
                val = _dot(h, wo_ref[:, col0:col0 + CH_TILE]) * decay
                if direction == 1 and half == 0:
                    val = jnp.where(no_tap, 0.0, val)
                l1_ref[:, cols] += jnp.sum(jnp.abs(val), axis=0, keepdims=True)
                for g in range(FILTER_GROUP):
                    out_ref[ct, half * quarter:(half + 1) * quarter,
                            g * CH_TILE:(g + 1) * CH_TILE] = (
                                val[g * quarter:(g + 1) * quarter].astype(BF16))


@functools.lru_cache(maxsize=None)
def _filter_position_tables(seq_len, d_a):
    steps = NB // FILTER_GROUP
    quarter = NB // 4
    step, g, q, half = np.meshgrid(np.arange(steps), np.arange(FILTER_GROUP), np.arange(quarter),
                                   np.arange(2), indexing="ij")
    pos = (NB * (half * quarter + q) + step * FILTER_GROUP + g).astype(np.float64)
    bands = (FILTER_EMB - 1) // 2
    t = pos / (seq_len - 1)
    w = 2.0 * math.pi * pos / seq_len
    f = np.linspace(1e-4, bands - 1, bands)
    z = np.zeros(pos.shape + (EMB_PAD,), np.float64)
    z[..., 0] = t
    z[..., 1:1 + bands] = np.cos(f * w[..., None])
    z[..., 1 + bands:1 + 2 * bands] = -np.sin(f * w[..., None])
    rows = steps * FILTER_GROUP * quarter
    deltas = np.abs(np.linspace(math.log(DECAY_TARGET) / SLOW_DECAY_PCT,
                                math.log(DECAY_TARGET) / FAST_DECAY_PCT, d_a))
    t = t.reshape(rows, 2)
    return (z.reshape(rows, 2 * EMB_PAD).astype(np.float32), t[:, 0:1].astype(np.float32),
            t[:, 1:2].astype(np.float32), deltas[None, :].astype(np.float32))


def _filter_mlp(w1, b1, f1, w2, b2, f2, w3, b3, f3, wo, seq_len, d_a):
    n_ct = d_a // CH_TILE
    steps = NB // FILTER_GROUP
    rows = FILTER_GROUP * NB // 4
    width = w2.shape[0]
    z2, ta, tb, deltas = _filter_position_tables(seq_len, d_a)
    const = lambda shape: pl.BlockSpec(shape, lambda i: (0,) * len(shape))
    taps = pl.BlockSpec((n_ct, NB // 2, FILTER_GROUP * CH_TILE), lambda i: (0, 0, i))
    return pl.pallas_call(
        functools.partial(_filter_mlp_kernel, d_a=d_a),
        grid=(steps,),
        in_specs=[
            pl.BlockSpec((rows, width), lambda i: (i, 0)),
            pl.BlockSpec((rows, 1), lambda i: (i, 0)),
            pl.BlockSpec((rows, 1), lambda i: (i, 0)),
            const((1, d_a)),
            const((width, width)), const((1, width)), const((1, width)),
            const((width, width)), const((1, width)), const((1, width)),
            const((width, width)), const((1, width)), const((1, width)),
            const(wo.shape),
        ],
        out_specs=[taps, taps, pl.BlockSpec((1, d_a), lambda i: (0, 0))],
        out_shape=[
            jax.ShapeDtypeStruct((n_ct, NB // 2, NB * CH_TILE), BF16),
            jax.ShapeDtypeStruct((n_ct, NB // 2, NB * CH_TILE), BF16),
            jax.ShapeDtypeStruct((1, d_a), F32),
        ],
        compiler_params=pltpu.CompilerParams(dimension_semantics=("arbitrary",)),
        name="filter_mlp",
    )(z2, ta, tb, deltas, w1, b1, f1, w2, b2, f2, w3, b3, f3, wo)


def _pack_rows(x):
    return pltpu.bitcast(x.astype(BF16), jnp.uint32)


def _store_slab(s_ref, n2, res):
    row = pl.multiple_of(n2 * SLAB_ROWS, 8)
    w = _pack_rows(res)
    s_ref[0, pl.ds(row, SLAB_ROWS), :] = w[:, :LANE]
    s_ref[1, pl.ds(row, SLAB_ROWS), :] = w[:, LANE:]


def _load_slab(s_ref, n2):
    row = pl.multiple_of(n2 * SLAB_ROWS, 8)
    w = jnp.concatenate(
        [s_ref[0, pl.ds(row, SLAB_ROWS), :], s_ref[1, pl.ds(row, SLAB_ROWS), :]], axis=1)
    return pltpu.bitcast(w, BF16)


def _stage_one_filter(kf_ref, kb_ref, g_ref, s_ref):
    def body(n2, carry):
        col = pl.multiple_of(n2 * CH_TILE, CH_TILE)
        mirror = pl.multiple_of(((NB - n2) & (NB - 1)) * CH_TILE, CH_TILE)
        src = jnp.concatenate(
            [kf_ref[:, pl.ds(col, CH_TILE)], kb_ref[:, pl.ds(mirror, CH_TILE)]], axis=0)
        _store_slab(s_ref, n2, _dot(g_ref[n2], src))
        return carry

    lax.fori_loop(0, NB, body, 0, unroll=STAGE_UNROLL)


def _store_time_rows(ref, words):
    half = NB // 2
    for j in range(words.shape[0] // half):
        ref[j * PAIR_PITCH:j * PAIR_PITCH + half, :] = words[j * half:(j + 1) * half]
        ref[j * PAIR_PITCH + half:(j + 1) * PAIR_PITCH, :] = (
            words[(j + 1) * half - PAD_ROWS:(j + 1) * half])


def _load_time_rows(ref):
    half = NB // 2
    return jnp.concatenate(
        [ref[j * PAIR_PITCH:j * PAIR_PITCH + half, :] for j in range(ref.shape[0] // PAIR_PITCH)],
        axis=0)


def _stage_one_packed(u_ref, g_ref, s_ref):
    def body(m, carry):
        w = jnp.concatenate(
            [u_ref.at[half][pl.ds(m, NB // 2, stride=PAIR_PITCH), :] for half in range(2)], axis=1)
        res = _dot(g_ref[m], pltpu.bitcast(w, BF16))
        _store_slab(s_ref, 2 * m, res[:STAGE_ROWS])
        _store_slab(s_ref, 2 * m + 1, res[STAGE_ROWS:])
        return carry

    lax.fori_loop(0, NB // 2, body, 0, unroll=STAGE_UNROLL // 2)


def _load_k1(s_ref, k1):
    w = jnp.concatenate(
        [s_ref.at[half][pl.ds(k1, NB, stride=SLAB_ROWS), :] for half in range(2)], axis=1)
    return pltpu.bitcast(w, BF16)


def _store_k1(s_ref, k1, val):
    w = _pack_rows(val)
    for half in range(2):
        s_ref.at[half][pl.ds(k1, NB, stride=SLAB_ROWS), :] = w[:, half * LANE:(half + 1) * LANE]


def _filter_fft_kernel(kf_ref, kb_ref, g_ref, f_ref, l1_ref, skip_ref, khat_ref, s_ref):
    _stage_one_filter(kf_ref, kb_ref, g_ref, s_ref)
    scale = 1.0 / (l1_ref[...] + EPS)
    is_real = lax.broadcasted_iota(jnp.int32, (2 * NB, CH_TILE), 0) < NB
    shift = jnp.where(is_real, skip_ref[...], 0.0)

    def body(k1, carry):
        x = _dot(f_ref[...], _load_k1(s_ref, k1))
        khat_ref[k1] = (x * scale + shift).astype(BF16)
        return carry

    lax.fori_loop(0, K1_COUNT, body, 0, unroll=SPECTRUM_UNROLL)


def _filter_fft(kf, kb, g_filt, f_fwd, l1, skip):
    n_ct = kf.shape[0]
    taps = pl.BlockSpec((None, NB // 2, NB * CH_TILE), lambda c: (c, 0, 0))
    return pl.pallas_call(
        _filter_fft_kernel,
        grid=(n_ct,),
        in_specs=[
            taps, taps,
            pl.BlockSpec((NB, STAGE_ROWS, NB), lambda c: (0, 0, 0),
                         pipeline_mode=pl.Buffered(1)),
            pl.BlockSpec((2 * NB, 2 * NB), lambda c: (0, 0)),
            pl.BlockSpec((1, CH_TILE), lambda c: (0, c)),
            pl.BlockSpec((1, CH_TILE), lambda c: (0, c)),
        ],
        out_specs=pl.BlockSpec((None, K1_COUNT, 2 * NB, CH_TILE), lambda c: (c, 0, 0, 0)),
        out_shape=jax.ShapeDtypeStruct((n_ct, K1_COUNT, 2 * NB, CH_TILE), BF16),
        scratch_shapes=[pltpu.VMEM((2, NB * SLAB_ROWS, LANE), jnp.uint32)],
        compiler_params=pltpu.CompilerParams(
            dimension_semantics=("arbitrary",), vmem_limit_bytes=VMEM_LIMIT),
        name="filter_fft",
    )(kf, kb, g_filt, f_fwd, l1, skip)


def _long_conv_kernel(u_ref, khat_ref, g_ref, ginv_ref, f_ref, finv_ref, o_ref, s_ref):
    _stage_one_packed(u_ref, g_ref, s_ref)

    def spectrum_body(k1, carry):
        x = _dot(f_ref[...], _load_k1(s_ref, k1)).astype(BF16)
        kh = khat_ref[k1]
        xr, xi = x[:NB], x[NB:]
        kr, ki = kh[:NB], kh[NB:]
        y = jnp.concatenate([xr * kr - xi * ki, xr * ki + xi * kr], axis=0)
        _store_k1(s_ref, k1, _dot(finv_ref[...], y))
        return carry

    lax.fori_loop(0, K1_COUNT, spectrum_body, 0, unroll=SPECTRUM_UNROLL)

    def inverse_body(m, carry):
        even = _dot(ginv_ref[2 * m], _load_slab(s_ref, 2 * m))
        odd = _dot(ginv_ref[2 * m + 1], _load_slab(s_ref, 2 * m + 1))
        w = pltpu.bitcast(pltpu.pack_elementwise([even, odd], packed_dtype=BF16), jnp.uint32)
        for half in range(2):
            o_ref.at[half][pl.ds(m, NB // 2, stride=PAIR_PITCH), :] = (
                w[:, half * LANE:(half + 1) * LANE])
        return carry

    lax.fori_loop(0, NB // 2, inverse_body, 0, unroll=STAGE_UNROLL // 2)

    for half in range(2):
        for r in range(NB // 2, PAIR_PITCH):
            o_ref.at[half][pl.ds(r, NB // 2, stride=PAIR_PITCH), :] = (
                o_ref.at[half][pl.ds(r - PAD_ROWS, NB // 2, stride=PAIR_PITCH), :])


def _long_conv(u, khat, g_pair, g_inv, f_fwd, f_inv):
    batch, n_slab, half_len, _ = u.shape
    slabs = CH_TILE // LANE
    n_ct = n_slab // slabs
    const = lambda shape: pl.BlockSpec(shape, lambda c, b: (0,) * len(shape),
                                       pipeline_mode=pl.Buffered(1))
    return pl.pallas_call(
        _long_conv_kernel,
        grid=(n_ct, batch),
        in_specs=[
            pl.BlockSpec((None, slabs, half_len, LANE), lambda c, b: (b, c, 0, 0)),
            pl.BlockSpec((None, K1_COUNT, 2 * NB, CH_TILE), lambda c, b: (c, 0, 0, 0)),
            const(g_pair.shape),
            const(g_inv.shape),
            const((2 * NB, 2 * NB)),
            const((2 * NB, 2 * NB)),
        ],
        out_specs=pl.BlockSpec((None, slabs, half_len, LANE), lambda c, b: (b, c, 0, 0)),
        out_shape=jax.ShapeDtypeStruct(u.shape, jnp.uint32),
        scratch_shapes=[pltpu.VMEM((2, NB * SLAB_ROWS, LANE), jnp.uint32)],
        compiler_params=pltpu.CompilerParams(
            dimension_semantics=("arbitrary", "arbitrary"), vmem_limit_bytes=VMEM_LIMIT),
        name="long_conv",
    )(u, khat, g_pair, g_inv, f_fwd, f_inv)


def _in_proj_kernel(x_ref, xp_ref, xn_ref, pnw_ref, w_ref, cw_ref, cb_ref,
                    lnw_ref, lnb_ref, sw_ref, sb_ref, u_ref, ga_ref, yb_ref,
                    *, n_ct, n_heads, head_dim, chunk):
    i = pl.program_id(1)
    last = pl.num_programs(1) - 1
    tm = x_ref.shape[0]
    pnw = pnw_ref[...]
    h_main = _rms_norm(x_ref[...], pnw)
    h_prev = jnp.where(i == 0, 0.0, _rms_norm(xp_ref[...], pnw))
    h_next = jnp.where(i == last, 0.0, _rms_norm(xn_ref[...], pnw))
    h_ext = jnp.concatenate([h_prev, h_main, h_next], axis=0).astype(BF16)
    h_bf = h_main.astype(BF16)
    rows = tm + 2 * HALO

    d_a = n_ct * CH_TILE
    d_b = n_heads * head_dim
    pair = 2 * head_dim
    n_chunks = tm // chunk
    slabs = CH_TILE // LANE
    big_dot = _dot

    def hyena_dots(ct):
        proj = [big_dot(h_ext, w_ref[:, which * d_a + ct * CH_TILE:
                                     which * d_a + (ct + 1) * CH_TILE]) for which in range(3)]
        return proj + [big_dot(h_bf, w_ref[:, 3 * d_a + ct * CH_TILE:3 * d_a + (ct + 1) * CH_TILE])]

    def hyena_finish(ct, proj):
        cols = slice(ct * CH_TILE, (ct + 1) * CH_TILE)
        conv = []
        for which in range(3):
            p = proj[which]
            cw = cw_ref[which * 3:(which + 1) * 3, cols]
            conv.append(pltpu.roll(p, 1, axis=0)[HALO:HALO + tm] * cw[0:1]
                        + p[HALO:HALO + tm] * cw[1:2]
                        + pltpu.roll(p, rows - 1, axis=0)[HALO:HALO + tm] * cw[2:3]
                        + cb_ref[which:which + 1, cols])
        gate = proj[3]
        u = conv[2] * conv[1]
        for half in range(slabs):
            _store_time_rows(u_ref.at[ct * slabs + half],
                             _pack_rows(u[:, half * LANE:(half + 1) * LANE]))
        ga_ref[:, cols] = (conv[0] * _silu(gate)).astype(BF16)

    def gmlp_dots(hp):
        return [big_dot(h_bf, w_ref[:, 4 * d_a + which * d_b + hp * pair:
                                    4 * d_a + which * d_b + (hp + 1) * pair]) for which in range(3)]

    def gmlp_finish(hp, q):
        for sub in range(2):
            hd = hp * 2 + sub
            lanes = slice(hd * head_dim, (hd + 1) * head_dim)
            lo = sub * head_dim
            su = q[0][:, lo:lo + head_dim]
            sv = q[1][:, lo:lo + head_dim]
            sg = q[2][:, lo:lo + head_dim]
            mu = jnp.mean(sv, axis=-1, keepdims=True)
            d = sv - mu
            var = jnp.mean(d * d, axis=-1, keepdims=True)
            vn = (d * lax.rsqrt(var + EPS) * lnw_ref[:, lanes] + lnb_ref[:, lanes]).astype(BF16)
            wide = jnp.concatenate(
                [vn[c * chunk:(c + 1) * chunk] for c in range(n_chunks)], axis=1)
            mixed = _dot(sw_ref[hd], wide) + sb_ref[hd]
            mixed = jnp.concatenate(
                [mixed[:, c * head_dim:(c + 1) * head_dim] for c in range(n_chunks)], axis=0)
            yb_ref[:, lanes] = (su * mixed * _silu(sg)).astype(BF16)

    units = []
    for k in range(max(n_ct, n_heads // 2)):
        if k < n_heads // 2:
            units.append((gmlp_dots, gmlp_finish, k))
        if k < n_ct:
            units.append((hyena_dots, hyena_finish, k))
    issued = [units[j][0](units[j][2]) for j in range(min(MATMUL_LOOKAHEAD, len(units)))]
    for idx, (_, finish, k) in enumerate(units):
        nxt = idx + MATMUL_LOOKAHEAD
        if nxt < len(units):
            issued.append(units[nxt][0](units[nxt][2]))
        finish(k, issued[idx])


def _in_proj(x, pre_norm_w, w, cw, cb, lnw, lnb, sw, sb, n_ct, n_heads, head_dim, chunk):
    batch, seq_len, d_model = x.shape
    tm = TOKEN_TILE
    steps = seq_len // tm
    halo_blocks = seq_len // HALO
    per_tile = tm // HALO
    const = lambda shape: pl.BlockSpec(shape, lambda b, i: (0,) * len(shape))
    const1 = lambda shape: pl.BlockSpec(shape, lambda b, i: (0,) * len(shape),
                                        pipeline_mode=pl.Buffered(1))
    d_b = n_heads * head_dim
    d_a = n_ct * CH_TILE
    return pl.pallas_call(
        functools.partial(_in_proj_kernel, n_ct=n_ct, n_heads=n_heads, head_dim=head_dim,
                          chunk=chunk),
        grid=(batch, steps),
        in_specs=[
            pl.BlockSpec((None, tm, d_model), lambda b, i: (b, i, 0)),
            pl.BlockSpec((None, HALO, d_model),
                         lambda b, i: (b, jnp.maximum(i * per_tile - 1, 0), 0)),
            pl.BlockSpec((None, HALO, d_model),
                         lambda b, i: (b, jnp.minimum((i + 1) * per_tile, halo_blocks - 1), 0)),
            const((1, d_model)),
            const1(w.shape),
            const(cw.shape), const(cb.shape), const(lnw.shape), const(lnb.shape),
            const(sw.shape), const(sb.shape),
        ],
        out_specs=[
            pl.BlockSpec((None, d_a // LANE, tm // NB * PAIR_PITCH, LANE),
                         lambda b, i: (b, 0, i, 0)),
            pl.BlockSpec((None, tm, d_a), lambda b, i: (b, i, 0)),
            pl.BlockSpec((None, tm, d_b), lambda b, i: (b, i, 0)),
        ],
        out_shape=[
            jax.ShapeDtypeStruct((batch, d_a // LANE, seq_len // NB * PAIR_PITCH, LANE),
                                 jnp.uint32),
            jax.ShapeDtypeStruct((batch, seq_len, d_a), BF16),
            jax.ShapeDtypeStruct((batch, seq_len, d_b), BF16),
        ],
        compiler_params=pltpu.CompilerParams(
            dimension_semantics=("arbitrary", "arbitrary"), vmem_limit_bytes=VMEM_LIMIT),
        name="in_proj",
    )(x, x, x, pre_norm_w, w, cw, cb, lnw, lnb, sw, sb)


def _out_proj_kernel(c_ref, ga_ref, yb_ref, x_ref, w_ref, pw_ref, o_ref):
    conv = jnp.concatenate(
        [pltpu.bitcast(_load_time_rows(c_ref.at[s]), BF16) for s in range(c_ref.shape[0])], axis=1)
    ya = (conv.astype(F32) * ga_ref[...].astype(F32)).astype(BF16)
    yc = jnp.concatenate([ya, yb_ref[...]], axis=1)
    y = _dot(yc, w_ref[...])
    o_ref[...] = x_ref[...] + _rms_norm(y, pw_ref[...])


def _out_proj(conv, ga, yb, x, w_out, post_norm_w):
    batch, seq_len, d_model = x.shape
    n_slab = conv.shape[1]
    tm = OUT_TOKEN_TILE
    return pl.pallas_call(
        _out_proj_kernel,
        grid=(batch, seq_len // tm),
        in_specs=[
            pl.BlockSpec((None, n_slab, tm // NB * PAIR_PITCH, LANE), lambda b, i: (b, 0, i, 0)),
            pl.BlockSpec((None, tm, ga.shape[-1]), lambda b, i: (b, i, 0)),
            pl.BlockSpec((None, tm, yb.shape[-1]), lambda b, i: (b, i, 0)),
            pl.BlockSpec((None, tm, d_model), lambda b, i: (b, i, 0)),
            pl.BlockSpec(w_out.shape, lambda b, i: (0, 0)),
            pl.BlockSpec((1, d_model), lambda b, i: (0, 0)),
        ],
        out_specs=pl.BlockSpec((None, tm, d_model), lambda b, i: (b, i, 0)),
        out_shape=jax.ShapeDtypeStruct(x.shape, x.dtype),
        compiler_params=pltpu.CompilerParams(
            dimension_semantics=("arbitrary", "arbitrary"), vmem_limit_bytes=VMEM_LIMIT),
        name="out_proj",
    )(conv, ga, yb, x, w_out, post_norm_w)


def kernel(x, pre_norm_w, w_in, conv_w, conv_b, filt_w1, filt_b1, filt_freq1, filt_w2, filt_b2, filt_freq2, filt_w3, filt_b3, filt_freq3, filt_w_out, hyena_skip, sgu_norm_w, sgu_norm_b, sgu_w, sgu_b, w_out, post_norm_w):
    batch, seq_len, d_model = x.shape
    d_a = hyena_skip.shape[0]
    n_heads, chunk, _ = sgu_w.shape
    d_b = sgu_norm_w.shape[0]
    head_dim = d_b // n_heads
    n_ct = d_a // CH_TILE
    assert 2 * seq_len == N_FFT and d_a % CH_TILE == 0 and head_dim == LANE
    assert seq_len % TOKEN_TILE == 0 and TOKEN_TILE % chunk == 0 and n_heads % 2 == 0
    assert seq_len % OUT_TOKEN_TILE == 0
    assert w_in.shape[1] == 4 * d_a + 3 * d_b

    g_filt, g_pair, g_inv, f_fwd, f_inv = (
        jnp.asarray(t).astype(BF16) for t in _dft_tables())

    row = lambda v: v.astype(F32).reshape(1, -1)
    twice = lambda w: jnp.kron(jnp.eye(2, dtype=F32), w.astype(F32))
    tiled = lambda v: jnp.tile(row(v), (1, 2))
    assert filt_w2.shape[0] == EMB_PAD
    w1 = jnp.pad(filt_w1.astype(F32), ((0, EMB_PAD - FILTER_EMB), (0, 0)))
    kf, kb, l1 = _filter_mlp(twice(w1), tiled(filt_b1), tiled(filt_freq1),
                             twice(filt_w2), tiled(filt_b2), tiled(filt_freq2),
                             twice(filt_w3), tiled(filt_b3), tiled(filt_freq3),
                             twice(filt_w_out).astype(BF16), seq_len, d_a)
    khat = _filter_fft(kf, kb, g_filt, f_fwd, l1, row(hyena_skip))

    cw = jnp.transpose(conv_w.astype(F32).reshape(3, 3, d_a), (1, 0, 2)).reshape(9, d_a)
    cb = conv_b.astype(F32).reshape(3, d_a)
    sb = jnp.broadcast_to(sgu_b.astype(F32)[:, :, None], (n_heads, chunk, 1))

    u, ga, yb = _in_proj(x, row(pre_norm_w), w_in.astype(BF16), cw, cb, row(sgu_norm_w),
                         row(sgu_norm_b), sgu_w.astype(BF16), sb,
                         n_ct, n_heads, head_dim, chunk)
    conv = _long_conv(u, khat, g_pair, g_inv, f_fwd, f_inv)
    return _out_proj(conv, ga, yb, x, w_out.astype(BF16), row(post_norm_w))
```

```python
import functools
import math

import numpy as np
import jax
import jax.numpy as jnp
from jax import lax
from jax.experimental import pallas as pl
from jax.experimental.pallas import tpu as pltpu

F32 = jnp.float32
BF16 = jnp.bfloat16

EPS = 1e-6
DECAY_TARGET = 1e-2
FAST_DECAY_PCT = 0.3
SLOW_DECAY_PCT = 1.5
FILTER_EMB = 33
EMB_PAD = 64

NB = 128
N_FFT = NB * NB
K1_COUNT = NB // 2 + 1
STAGE_ROWS = 144
SLAB_ROWS = STAGE_ROWS // 2
PAD_ROWS = 8
PAIR_PITCH = NB // 2 + PAD_ROWS
CH_TILE = 256
LANE = 128
TOKEN_TILE = 1024
OUT_TOKEN_TILE = 1024
OUT_INPUT_BUFFERS = 3
HALO = 8
FILTER_GROUP = 16
MATMUL_LOOKAHEAD = 1
V7X_VMEM_BYTES = 64 * 1024 * 1024
VMEM_LIMIT = V7X_VMEM_BYTES - 4 * 1024 * 1024


@functools.lru_cache(maxsize=None)
def _dft_tables():
    k1 = np.arange(K1_COUNT, dtype=np.int64)
    n1 = np.arange(NB, dtype=np.int64)
    n2 = np.arange(NB, dtype=np.int64)
    phase = (k1[None, :, None] * (NB * n1[None, None, :] + n2[:, None, None])) % N_FFT
    theta = 2.0 * np.pi * phase.astype(np.float64) / N_FFT
    g = np.zeros((NB, STAGE_ROWS, NB), np.float64)
    g[:, 0:2 * K1_COUNT:2, :] = np.cos(theta)
    g[:, 1:2 * K1_COUNT:2, :] = -np.sin(theta)
    herm = np.full((K1_COUNT,), 2.0)
    herm[0] = 1.0
    herm[-1] = 1.0
    row_w = np.zeros((STAGE_ROWS,), np.float64)
    row_w[:2 * K1_COUNT] = np.repeat(herm / N_FFT, 2)
    g_inv = np.transpose(g[:, :, :NB // 2] * row_w[None, :, None], (0, 2, 1))
    a = np.arange(NB, dtype=np.int64)
    ang = 2.0 * np.pi * ((a[:, None] * a[None, :]) % NB).astype(np.float64) / NB
    c, s = np.cos(ang), np.sin(ang)
    f_fwd = np.block([[c, s], [-s, c]]).reshape(2 * NB, 2, NB)
    f_fwd = np.transpose(f_fwd, (0, 2, 1)).reshape(2 * NB, 2 * NB)
    f_inv = np.block([[c, -s], [s, c]]).reshape(2, NB, 2 * NB)
    f_inv = np.transpose(f_inv, (1, 0, 2)).reshape(2 * NB, 2 * NB)
    half = NB // 2
    g_pair = np.zeros((half, 2, STAGE_ROWS, half, 2), np.float64)
    for p in range(2):
        g_pair[:, p, :, :, p] = g[p::2, :, :half]
    g_pair = g_pair.reshape(half, 2 * STAGE_ROWS, NB)
    g_filt = g.copy()
    g_filt[1:, :, half:] = g[1:, :, :half - 1:-1]
    g_filt[0, :, half] = 0.0
    g_filt[0, :, half + 1:] = g[0, :, :half:-1]
    return tuple(t.astype(np.float32) for t in (g_filt, g_pair, g_inv, f_fwd, f_inv))


def _dot(a, b):
    return jnp.dot(a, b, preferred_element_type=F32)


def _dot_split(a, b):
    a_hi = a.astype(BF16)
    b_hi = b.astype(BF16)
    a_lo = (a - a_hi.astype(F32)).astype(BF16)
    b_lo = (b - b_hi.astype(F32)).astype(BF16)
    return _dot(a_hi, b_hi) + _dot(a_lo, b_hi) + _dot(a_hi, b_lo)


def _silu(x):
    return x / (1.0 + jnp.exp(-x))


def _rms_norm(x, w):
    return x * lax.rsqrt(jnp.mean(x * x, axis=-1, keepdims=True) + EPS) * w


def _filter_mlp_kernel(z_ref, ta_ref, tb_ref, delta_ref, w1_ref, b1_ref, f1_ref, w2_ref, b2_ref,
                       f2_ref, w3_ref, b3_ref, f3_ref, wo_ref, win_ref, wout_ref,
                       kf_ref, kb_ref, l1_ref, win_bf_ref, wout_bf_ref, *, d_a):
    win_bf_ref[...] = win_ref[...].astype(BF16)
    wout_bf_ref[...] = wout_ref[...].astype(BF16)
    step = pl.program_id(0)
    n_ct = kf_ref.shape[0]
    quarter = NB // 4
    h = jnp.sin(f1_ref[...] * (_dot_split(z_ref[...], w1_ref[...]) + b1_ref[...]))
    h = jnp.sin(f2_ref[...] * (_dot_split(h, w2_ref[...]) + b2_ref[...]))
    h = jnp.sin(f3_ref[...] * (_dot_split(h, w3_ref[...]) + b3_ref[...])).astype(BF16)

    @pl.when(step == 0)
    def _():
        l1_ref[...] = jnp.zeros_like(l1_ref)

    row = lax.broadcasted_iota(jnp.int32, (h.shape[0], CH_TILE), 0)
    no_tap = jnp.logical_and(row == 0, step == 0)
    for half, t_ref in enumerate((ta_ref, tb_ref)):
        for ct in range(n_ct):
            cols = slice(ct * CH_TILE, (ct + 1) * CH_TILE)
            decay = jnp.exp(-t_ref[...] * delta_ref[:, cols])
            for direction, out_ref in enumerate((kf_ref, kb_ref)):
                col0 = (half * 2 + direction) * d_a + ct * CH_TILE
                val = _dot(h, wo_ref[:, col0:col0 + CH_TILE]) * decay
                if direction == 1 and half == 0:
                    val = jnp.where(no_tap, 0.0, val)
                l1_ref[:, cols] += jnp.sum(jnp.abs(val), axis=0, keepdims=True)
                for g in range(FILTER_GROUP):
                    out_ref[ct, half * quarter:(half + 1) * quarter,
                            g * CH_TILE:(g + 1) * CH_TILE] = (
                                val[g * quarter:(g + 1) * quarter].astype(BF16))


@functools.lru_cache(maxsize=None)
def _filter_position_tables(seq_len, d_a):
    steps = NB // FILTER_GROUP
    quarter = NB // 4
    step, g, q, half = np.meshgrid(np.arange(steps), np.arange(FILTER_GROUP), np.arange(quarter),
                                   np.arange(2), indexing="ij")
    pos = (NB * (half * quarter + q) + step * FILTER_GROUP + g).astype(np.float64)
    bands = (FILTER_EMB - 1) // 2
    t = pos / (seq_len - 1)
    w = 2.0 * math.pi * pos / seq_len
    f = np.linspace(1e-4, bands - 1, bands)
    z = np.zeros(pos.shape + (EMB_PAD,), np.float64)
    z[..., 0] = t
    z[..., 1:1 + bands] = np.cos(f * w[..., None])
    z[..., 1 + bands:1 + 2 * bands] = -np.sin(f * w[..., None])
    rows = steps * FILTER_GROUP * quarter
    deltas = np.abs(np.linspace(math.log(DECAY_TARGET) / SLOW_DECAY_PCT,
                                math.log(DECAY_TARGET) / FAST_DECAY_PCT, d_a))
    t = t.reshape(rows, 2)
    return (z.reshape(rows, 2 * EMB_PAD).astype(np.float32), t[:, 0:1].astype(np.float32),
            t[:, 1:2].astype(np.float32), deltas[None, :].astype(np.float32))


def _filter_mlp(w1, b1, f1, w2, b2, f2, w3, b3, f3, wo, w_in, w_out, seq_len, d_a):
    n_ct = d_a // CH_TILE
    steps = NB // FILTER_GROUP
    assert w_in.shape[0] % (16 * steps) == 0 and w_out.shape[0] % (16 * steps) == 0
    slab = lambda w: pl.BlockSpec((w.shape[0] // steps, w.shape[1]), lambda i: (i, 0))
    rows = FILTER_GROUP * NB // 4
    width = w2.shape[0]
    z2, ta, tb, deltas = _filter_position_tables(seq_len, d_a)
    const = lambda shape: pl.BlockSpec(shape, lambda i: (0,) * len(shape))
    taps = pl.BlockSpec((n_ct, NB // 2, FILTER_GROUP * CH_TILE), lambda i: (0, 0, i))
    return pl.pallas_call(
        functools.partial(_filter_mlp_kernel, d_a=d_a),
        grid=(steps,),
        in_specs=[
            pl.BlockSpec((rows, width), lambda i: (i, 0)),
            pl.BlockSpec((rows, 1), lambda i: (i, 0)),
            pl.BlockSpec((rows, 1), lambda i: (i, 0)),
            const((1, d_a)),
            const((width, width)), const((1, width)), const((1, width)),
            const((width, width)), const((1, width)), const((1, width)),
            const((width, width)), const((1, width)), const((1, width)),
            const(wo.shape),
            slab(w_in), slab(w_out),
        ],
        out_specs=[taps, taps, pl.BlockSpec((1, d_a), lambda i: (0, 0)), slab(w_in), slab(w_out)],
        out_shape=[
            jax.ShapeDtypeStruct((n_ct, NB // 2, NB * CH_TILE), BF16),
            jax.ShapeDtypeStruct((n_ct, NB // 2, NB * CH_TILE), BF16),
            jax.ShapeDtypeStruct((1, d_a), F32),
            jax.ShapeDtypeStruct(w_in.shape, BF16),
            jax.ShapeDtypeStruct(w_out.shape, BF16),
        ],
        compiler_params=pltpu.CompilerParams(dimension_semantics=("arbitrary",)),
        name="filter_mlp",
    )(z2, ta, tb, deltas, w1, b1, f1, w2, b2, f2, w3, b3, f3, wo, w_in, w_out)


def _pack_rows(x):
    return pltpu.bitcast(x.astype(BF16), jnp.uint32)


def _store_slab(s_ref, n2, res):
    rows = slice(n2 * SLAB_ROWS, (n2 + 1) * SLAB_ROWS)
    w = _pack_rows(res)
    s_ref[0, rows, :] = w[:, :LANE]
    s_ref[1, rows, :] = w[:, LANE:]


def _load_slab(s_ref, n2):
    rows = slice(n2 * SLAB_ROWS, (n2 + 1) * SLAB_ROWS)
    w = jnp.concatenate([s_ref[0, rows, :], s_ref[1, rows, :]], axis=1)
    return pltpu.bitcast(w, BF16)


def _stage_one_filter(kf_ref, kb_ref, g_ref, s_ref):
    for n2 in range(NB):
        col = n2 * CH_TILE
        mirror = ((NB - n2) % NB) * CH_TILE
        src = jnp.concatenate(
            [kf_ref[:, col:col + CH_TILE], kb_ref[:, mirror:mirror + CH_TILE]], axis=0)
        _store_slab(s_ref, n2, _dot(g_ref[n2], src))


def _store_time_rows(ref, words):
    half = NB // 2
    for j in range(words.shape[0] // half):
        ref[j * PAIR_PITCH:j * PAIR_PITCH + half, :] = words[j * half:(j + 1) * half]
        ref[j * PAIR_PITCH + half:(j + 1) * PAIR_PITCH, :] = (
            words[(j + 1) * half - PAD_ROWS:(j + 1) * half])


def _load_time_rows(ref):
    half = NB // 2
    return jnp.concatenate(
        [ref[j * PAIR_PITCH:j * PAIR_PITCH + half, :] for j in range(ref.shape[0] // PAIR_PITCH)],
        axis=0)


def _stage_one_packed(u_ref, g_ref, s_ref):
    for m in range(NB // 2):
        w = jnp.concatenate(
            [u_ref.at[half][pl.ds(m, NB // 2, stride=PAIR_PITCH), :] for half in range(2)], axis=1)
        res = _dot(g_ref[m], pltpu.bitcast(w, BF16))
        _store_slab(s_ref, 2 * m, res[:STAGE_ROWS])
        _store_slab(s_ref, 2 * m + 1, res[STAGE_ROWS:])


def _load_k1(s_ref, k1):
    w = jnp.concatenate(
        [s_ref.at[half][pl.ds(k1, NB, stride=SLAB_ROWS), :] for half in range(2)], axis=1)
    return pltpu.bitcast(w, BF16)


def _store_k1(s_ref, k1, val):
    w = _pack_rows(val)
    for half in range(2):
        s_ref.at[half][pl.ds(k1, NB, stride=SLAB_ROWS), :] = w[:, half * LANE:(half + 1) * LANE]


def _filter_fft_kernel(kf_ref, kb_ref, g_ref, f_ref, l1_ref, skip_ref, khat_ref, s_ref):
    _stage_one_filter(kf_ref, kb_ref, g_ref, s_ref)
    scale = 1.0 / (l1_ref[...] + EPS)
    is_real = lax.broadcasted_iota(jnp.int32, (2 * NB, CH_TILE), 0) < NB
    shift = jnp.where(is_real, skip_ref[...], 0.0)

    for k1 in range(K1_COUNT):
        x = _dot(f_ref[...], _load_k1(s_ref, k1))
        khat_ref[k1] = (x * scale + shift).astype(BF16)


def _filter_fft(kf, kb, g_filt, f_fwd, l1, skip):
    n_ct = kf.shape[0]
    taps = pl.BlockSpec((None, NB // 2, NB * CH_TILE), lambda c: (c, 0, 0))
    return pl.pallas_call(
        _filter_fft_kernel,
        grid=(n_ct,),
        in_specs=[
            taps, taps,
            pl.BlockSpec((NB, STAGE_ROWS, NB), lambda c: (0, 0, 0),
                         pipeline_mode=pl.Buffered(1)),
            pl.BlockSpec((2 * NB, 2 * NB), lambda c: (0, 0)),
            pl.BlockSpec((1, CH_TILE), lambda c: (0, c)),
            pl.BlockSpec((1, CH_TILE), lambda c: (0, c)),
        ],
        out_specs=pl.BlockSpec((None, K1_COUNT, 2 * NB, CH_TILE), lambda c: (c, 0, 0, 0)),
        out_shape=jax.ShapeDtypeStruct((n_ct, K1_COUNT, 2 * NB, CH_TILE), BF16),
        scratch_shapes=[pltpu.VMEM((2, NB * SLAB_ROWS, LANE), jnp.uint32)],
        compiler_params=pltpu.CompilerParams(
            dimension_semantics=("arbitrary",), vmem_limit_bytes=VMEM_LIMIT),
        name="filter_fft",
    )(kf, kb, g_filt, f_fwd, l1, skip)


def _long_conv_kernel(u_ref, khat_ref, g_ref, ginv_ref, f_ref, finv_ref, o_ref, s_ref):
    _stage_one_packed(u_ref, g_ref, s_ref)

    for k1 in range(K1_COUNT):
        x = _dot(f_ref[...], _load_k1(s_ref, k1)).astype(BF16)
        kh = khat_ref[k1]
        xr, xi = x[:NB], x[NB:]
        kr, ki = kh[:NB], kh[NB:]
        y = jnp.concatenate([xr * kr - xi * ki, xr * ki + xi * kr], axis=0)
        _store_k1(s_ref, k1, _dot(finv_ref[...], y))

    for m in range(NB // 2):
        even = _dot(ginv_ref[2 * m], _load_slab(s_ref, 2 * m))
        odd = _dot(ginv_ref[2 * m + 1], _load_slab(s_ref, 2 * m + 1))
        w = pltpu.bitcast(pltpu.pack_elementwise([even, odd], packed_dtype=BF16), jnp.uint32)
        for half in range(2):
            o_ref.at[half][pl.ds(m, NB // 2, stride=PAIR_PITCH), :] = (
                w[:, half * LANE:(half + 1) * LANE])

    for half in range(2):
        for r in range(NB // 2, PAIR_PITCH):
            o_ref.at[half][pl.ds(r, NB // 2, stride=PAIR_PITCH), :] = (
                o_ref.at[half][pl.ds(r - PAD_ROWS, NB // 2, stride=PAIR_PITCH), :])


def _long_conv(u, khat, g_pair, g_inv, f_fwd, f_inv):
    batch, n_slab, half_len, _ = u.shape
    slabs = CH_TILE // LANE
    n_ct = n_slab // slabs
    const = lambda shape: pl.BlockSpec(shape, lambda c, b: (0,) * len(shape),
                                       pipeline_mode=pl.Buffered(1))
    return pl.pallas_call(
        _long_conv_kernel,
        grid=(n_ct, batch),
        in_specs=[
            pl.BlockSpec((None, slabs, half_len, LANE), lambda c, b: (b, c, 0, 0)),
            pl.BlockSpec((None, K1_COUNT, 2 * NB, CH_TILE), lambda c, b: (c, 0, 0, 0)),
            const(g_pair.shape),
            const(g_inv.shape),
            const((2 * NB, 2 * NB)),
            const((2 * NB, 2 * NB)),
        ],
        out_specs=pl.BlockSpec((None, slabs, half_len, LANE), lambda c, b: (b, c, 0, 0)),
        out_shape=jax.ShapeDtypeStruct(u.shape, jnp.uint32),
        scratch_shapes=[pltpu.VMEM((2, NB * SLAB_ROWS, LANE), jnp.uint32)],
        compiler_params=pltpu.CompilerParams(
            dimension_semantics=("arbitrary", "arbitrary"), vmem_limit_bytes=VMEM_LIMIT),
        name="long_conv",
    )(u, khat, g_pair, g_inv, f_fwd, f_inv)


def _in_proj_kernel(x_ref, xp_ref, xn_ref, pnw_ref, w_ref, cw_ref, cb_ref,
                    lnw_ref, lnb_ref, sw_ref, sb_ref, u_ref, ga_ref, yb_ref,
                    *, n_ct, n_heads, head_dim, chunk):
    i = pl.program_id(1)
    last = pl.num_programs(1) - 1
    tm = x_ref.shape[0]
    pnw = pnw_ref[...]
    h_main = _rms_norm(x_ref[...], pnw)
    h_prev = jnp.where(i == 0, 0.0, _rms_norm(xp_ref[...], pnw))
    h_next = jnp.where(i == last, 0.0, _rms_norm(xn_ref[...], pnw))
    h_ext = jnp.concatenate([h_prev, h_main, h_next], axis=0).astype(BF16)
    h_bf = h_main.astype(BF16)
    rows = tm + 2 * HALO

    d_a = n_ct * CH_TILE
    d_b = n_heads * head_dim
    pair = 2 * head_dim
    n_chunks = tm // chunk
    slabs = CH_TILE // LANE

    def hyena_dots(ct):
        proj = [_dot(h_ext, w_ref[:, which * d_a + ct * CH_TILE:
                                     which * d_a + (ct + 1) * CH_TILE]) for which in range(3)]
        return proj + [_dot(h_bf, w_ref[:, 3 * d_a + ct * CH_TILE:3 * d_a + (ct + 1) * CH_TILE])]

    def hyena_finish(ct, proj):
        cols = slice(ct * CH_TILE, (ct + 1) * CH_TILE)
        conv = []
        for which in range(3):
            p = proj[which]
            cw = cw_ref[which * 3:(which + 1) * 3, cols]
            conv.append(pltpu.roll(p, 1, axis=0)[HALO:HALO + tm] * cw[0:1]
                        + p[HALO:HALO + tm] * cw[1:2]
                        + pltpu.roll(p, rows - 1, axis=0)[HALO:HALO + tm] * cw[2:3]
                        + cb_ref[which:which + 1, cols])
        gate = proj[3]
        u = conv[2] * conv[1]
        for half in range(slabs):
            _store_time_rows(u_ref.at[ct * slabs + half],
                             _pack_rows(u[:, half * LANE:(half + 1) * LANE]))
        ga_ref[:, cols] = (conv[0] * _silu(gate)).astype(BF16)

    def gmlp_dots(hp):
        return [_dot(h_bf, w_ref[:, 4 * d_a + which * d_b + hp * pair:
                                 4 * d_a + which * d_b + (hp + 1) * pair]) for which in range(3)]

    def gmlp_finish(hp, q):
        for sub in range(2):
            hd = hp * 2 + sub
            lanes = slice(hd * head_dim, (hd + 1) * head_dim)
            lo = sub * head_dim
            su = q[0][:, lo:lo + head_dim]
            sv = q[1][:, lo:lo + head_dim]
            sg = q[2][:, lo:lo + head_dim]
            mu = jnp.mean(sv, axis=-1, keepdims=True)
            d = sv - mu
            var = jnp.mean(d * d, axis=-1, keepdims=True)
            vn = (d * lax.rsqrt(var + EPS) * lnw_ref[:, lanes] + lnb_ref[:, lanes]).astype(BF16)
            wide = jnp.concatenate(
                [vn[c * chunk:(c + 1) * chunk] for c in range(n_chunks)], axis=1)
            mixed = _dot(sw_ref[hd], wide) + sb_ref[hd]
            mixed = jnp.concatenate(
                [mixed[:, c * head_dim:(c + 1) * head_dim] for c in range(n_chunks)], axis=0)
            yb_ref[:, lanes] = (su * mixed * _silu(sg)).astype(BF16)

    units = []
    for k in range(max(n_ct, n_heads // 2)):
        if k < n_heads // 2:
            units.append((gmlp_dots, gmlp_finish, k))
        if k < n_ct:
            units.append((hyena_dots, hyena_finish, k))
    issued = [units[j][0](units[j][2]) for j in range(min(MATMUL_LOOKAHEAD, len(units)))]
    for idx, (_, finish, k) in enumerate(units):
        nxt = idx + MATMUL_LOOKAHEAD
        if nxt < len(units):
            issued.append(units[nxt][0](units[nxt][2]))
        finish(k, issued[idx])


def _in_proj(x, pre_norm_w, w, cw, cb, lnw, lnb, sw, sb, n_ct, n_heads, head_dim, chunk):
    batch, seq_len, d_model = x.shape
    tm = TOKEN_TILE
    steps = seq_len // tm
    halo_blocks = seq_len // HALO
    per_tile = tm // HALO
    const = lambda shape: pl.BlockSpec(shape, lambda b, i: (0,) * len(shape))
    const1 = lambda shape: pl.BlockSpec(shape, lambda b, i: (0,) * len(shape),
                                        pipeline_mode=pl.Buffered(1))
    d_b = n_heads * head_dim
    d_a = n_ct * CH_TILE
    return pl.pallas_call(
        functools.partial(_in_proj_kernel, n_ct=n_ct, n_heads=n_heads, head_dim=head_dim,
                          chunk=chunk),
        grid=(batch, steps),
        in_specs=[
            pl.BlockSpec((None, tm, d_model), lambda b, i: (b, i, 0)),
            pl.BlockSpec((None, HALO, d_model),
                         lambda b, i: (b, jnp.maximum(i * per_tile - 1, 0), 0)),
            pl.BlockSpec((None, HALO, d_model),
                         lambda b, i: (b, jnp.minimum((i + 1) * per_tile, halo_blocks - 1), 0)),
            const((1, d_model)),
            const1(w.shape),
            const(cw.shape), const(cb.shape), const(lnw.shape), const(lnb.shape),
            const(sw.shape), const(sb.shape),
        ],
        out_specs=[
            pl.BlockSpec((None, d_a // LANE, tm // NB * PAIR_PITCH, LANE),
                         lambda b, i: (b, 0, i, 0)),
            pl.BlockSpec((None, tm, d_a), lambda b, i: (b, i, 0)),
            pl.BlockSpec((None, tm, d_b), lambda b, i: (b, i, 0)),
        ],
        out_shape=[
            jax.ShapeDtypeStruct((batch, d_a // LANE, seq_len // NB * PAIR_PITCH, LANE),
                                 jnp.uint32),
            jax.ShapeDtypeStruct((batch, seq_len, d_a), BF16),
            jax.ShapeDtypeStruct((batch, seq_len, d_b), BF16),
        ],
        compiler_params=pltpu.CompilerParams(
            dimension_semantics=("arbitrary", "arbitrary"), vmem_limit_bytes=VMEM_LIMIT),
        name="in_proj",
    )(x, x, x, pre_norm_w, w, cw, cb, lnw, lnb, sw, sb)


def _out_proj_kernel(c_ref, ga_ref, yb_ref, x_ref, w_ref, pw_ref, o_ref):
    conv = jnp.concatenate(
        [pltpu.bitcast(_load_time_rows(c_ref.at[s]), BF16) for s in range(c_ref.shape[0])], axis=1)
    ya = (conv.astype(F32) * ga_ref[...].astype(F32)).astype(BF16)
    yc = jnp.concatenate([ya, yb_ref[...]], axis=1)
    y = _dot(yc, w_ref[...])
    o_ref[...] = x_ref[...] + _rms_norm(y, pw_ref[...])


def _out_proj(conv, ga, yb, x, w_out, post_norm_w):
    batch, seq_len, d_model = x.shape
    n_slab = conv.shape[1]
    tm = OUT_TOKEN_TILE
    deep = pl.Buffered(OUT_INPUT_BUFFERS)
    in_specs = [
        pl.BlockSpec((None, n_slab, tm // NB * PAIR_PITCH, LANE), lambda b, i: (b, 0, i, 0),
                     pipeline_mode=deep),
        pl.BlockSpec((None, tm, ga.shape[-1]), lambda b, i: (b, i, 0), pipeline_mode=deep),
        pl.BlockSpec((None, tm, yb.shape[-1]), lambda b, i: (b, i, 0), pipeline_mode=deep),
        pl.BlockSpec((None, tm, d_model), lambda b, i: (b, i, 0), pipeline_mode=deep),
        pl.BlockSpec(w_out.shape, lambda b, i: (0, 0)),
        pl.BlockSpec((1, d_model), lambda b, i: (0, 0)),
    ]
    out_spec = pl.BlockSpec((None, tm, d_model), lambda b, i: (b, i, 0))

    def pipelined(*refs):
        pltpu.emit_pipeline(_out_proj_kernel, grid=(batch, seq_len // tm),
                            in_specs=in_specs, out_specs=[out_spec])(*refs)

    anywhere = pl.BlockSpec(memory_space=pl.ANY)
    return pl.pallas_call(
        pipelined,
        in_specs=[anywhere] * 6,
        out_specs=anywhere,
        out_shape=jax.ShapeDtypeStruct(x.shape, x.dtype),
        compiler_params=pltpu.CompilerParams(vmem_limit_bytes=VMEM_LIMIT),
        name="out_proj",
    )(conv, ga, yb, x, w_out, post_norm_w)


def kernel(x, pre_norm_w, w_in, conv_w, conv_b, filt_w1, filt_b1, filt_freq1, filt_w2, filt_b2, filt_freq2, filt_w3, filt_b3, filt_freq3, filt_w_out, hyena_skip, sgu_norm_w, sgu_norm_b, sgu_w, sgu_b, w_out, post_norm_w):
    batch, seq_len, d_model = x.shape
    d_a = hyena_skip.shape[0]
    n_heads, chunk, _ = sgu_w.shape
    d_b = sgu_norm_w.shape[0]
    head_dim = d_b // n_heads
    n_ct = d_a // CH_TILE
    assert 2 * seq_len == N_FFT and d_a % CH_TILE == 0 and head_dim == LANE
    assert seq_len % TOKEN_TILE == 0 and TOKEN_TILE % chunk == 0 and n_heads % 2 == 0
    assert seq_len % OUT_TOKEN_TILE == 0
    assert w_in.shape[1] == 4 * d_a + 3 * d_b

    g_filt, g_pair, g_inv, f_fwd, f_inv = (
        jnp.asarray(t).astype(BF16) for t in _dft_tables())

    row = lambda v: v.astype(F32).reshape(1, -1)
    twice = lambda w: jnp.kron(jnp.eye(2, dtype=F32), w.astype(F32))
    tiled = lambda v: jnp.tile(row(v), (1, 2))
    assert filt_w2.shape[0] == EMB_PAD
    w1 = jnp.pad(filt_w1.astype(F32), ((0, EMB_PAD - FILTER_EMB), (0, 0)))
    kf, kb, l1, w_in_bf, w_out_bf = _filter_mlp(
        twice(w1), tiled(filt_b1), tiled(filt_freq1),
        twice(filt_w2), tiled(filt_b2), tiled(filt_freq2),
        twice(filt_w3), tiled(filt_b3), tiled(filt_freq3),
        twice(filt_w_out).astype(BF16), w_in.astype(F32), w_out.astype(F32), seq_len, d_a)
    khat = _filter_fft(kf, kb, g_filt, f_fwd, l1, row(hyena_skip))

    cw = jnp.transpose(conv_w.astype(F32).reshape(3, 3, d_a), (1, 0, 2)).reshape(9, d_a)
    cb = conv_b.astype(F32).reshape(3, d_a)
    sb = jnp.broadcast_to(sgu_b.astype(F32)[:, :, None], (n_heads, chunk, 1))

    u, ga, yb = _in_proj(x, row(pre_norm_w), w_in_bf, cw, cb, row(sgu_norm_w),
                         row(sgu_norm_b), sgu_w.astype(BF16), sb,
                         n_ct, n_heads, head_dim, chunk)
    conv = _long_conv(u, khat, g_pair, g_inv, f_fwd, f_inv)
    return _out_proj(conv, ga, yb, x, w_out_bf, row(post_norm_w))
```
